```python
import jax, jax.numpy as jnp
from jax import lax
import numpy as np

D_MODEL = 1024
BATCH = 4
SEQ = 4096
DEPTH = 2
DEC_BATCH = 128
DEC_SEQ = 1
PAST_LEN = 16384
PAGE_SIZE = 128

GLA_H = 4
GLA_DK = 64
GLA_DV = 128
GLA_RANK = 16
GLA_TAU = 16.0
GLA_CHUNK = 64
NSA_H = 8
NSA_D = 64
NSA_SCALE = NSA_D ** -0.5
CMP_BLOCK = 32
SEL_BLOCK = 64
CMP_PER_SEL = SEL_BLOCK // CMP_BLOCK
SEL_TOPK = 16
WINDOW = 512
QBLOCK = 128
RG_W = 512
RG_BLOCKS = 8
RG_BD = RG_W // RG_BLOCKS
CONV_W = 4
RG_C = 8.0
MLA_H = 8
MLA_QR = 384
MLA_KVR = 256
MLA_DN = 64
MLA_DR = 32
MLA_DV = 64
MLA_SCALE = (MLA_DN + MLA_DR) ** -0.5
ROPE_BASE = 10000.0
N_BRANCH = 4
BRANCH_W = 512
D_FF = 4 * D_MODEL
DN_ALPHA = (2 * DEPTH) ** 0.25
DN_BETA = (8 * DEPTH) ** -0.25
LN_EPS = 1e-5
IN_SIZES = (GLA_H * GLA_DK, GLA_H * GLA_DK, GLA_H * GLA_DV, GLA_RANK, GLA_H * GLA_DV,
            NSA_H * NSA_D, 6 * NSA_D, 3 * NSA_H, RG_W, RG_W, MLA_QR, MLA_KVR, MLA_DR, N_BRANCH * D_MODEL)
IN_SPLITS = tuple(int(v) for v in np.cumsum(IN_SIZES)[:-1])
D_IN = sum(IN_SIZES)

kernel_name = 'hybrid_gla_nsa_rglru_mla_step'


def rms_norm(x, g, eps=1e-6):
    xf = x.astype(jnp.float32)
    y = xf * lax.rsqrt(jnp.mean(xf * xf, axis=-1, keepdims=True) + eps)
    return (y * g.astype(jnp.float32)).astype(x.dtype)


def layer_norm(x, g, b):
    xf = x.astype(jnp.float32)
    mu = jnp.mean(xf, axis=-1, keepdims=True)
    var = jnp.mean(jnp.square(xf - mu), axis=-1, keepdims=True)
    return ((xf - mu) * lax.rsqrt(var + LN_EPS) * g.astype(jnp.float32) + b.astype(jnp.float32)).astype(x.dtype)


def masked_softmax(s, mask):
    s = jnp.where(mask, s.astype(jnp.float32), -jnp.inf)
    m = jnp.max(s, axis=-1, keepdims=True)
    m = jnp.where(jnp.isfinite(m), m, 0.0)
    e = jnp.exp(s - m)
    d = jnp.sum(e, axis=-1, keepdims=True)
    return e / jnp.where(d > 0, d, 1.0)


def alibi_slopes(n):
    return jnp.exp2(-8.0 * jnp.arange(1, n + 1, dtype=jnp.float32) / n)


def rope(x, pos):
    half = x.shape[-1] // 2
    freq = ROPE_BASE ** (-jnp.arange(half, dtype=jnp.float32) / half)
    ang = pos.astype(jnp.float32)[:, None, None] * freq
    cos, sin = jnp.cos(ang).astype(x.dtype), jnp.sin(ang).astype(x.dtype)
    x1, x2 = x[..., :half], x[..., half:]
    return jnp.concatenate([x1 * cos - x2 * sin, x1 * sin + x2 * cos], axis=-1)


def _qblock(lq):
    return QBLOCK if lq % QBLOCK == 0 else lq


def _linrec(e1, e2):
    a1, b1 = e1
    a2, b2 = e2
    return a1 * a2, a2 * b1 + b2


def gla_scan(q, k, v, log_a, s0):
    B, L, H, _ = q.shape
    DV = v.shape[-1]
    c = GLA_CHUNK if L % GLA_CHUNK == 0 else L
    n = L // c
    tri = jnp.tril(jnp.ones((c, c), dtype=bool))[None, :, :, None, None]

    def chunks(t):
        return t.astype(jnp.float32).reshape(B, n, c, H, t.shape[-1]).swapaxes(0, 1)

    def step(s, inp):
        qc, kc, vc, gc = inp
        b = jnp.cumsum(gc, axis=1)
        o_inter = jnp.einsum('bthd,bhde->bthe', qc * jnp.exp(b), s)
        decay = jnp.exp(jnp.where(tri, b[:, :, None] - b[:, None, :], -jnp.inf))
        att = jnp.einsum('btshd,bshd->bhts', qc[:, :, None] * decay, kc)
        o_intra = jnp.einsum('bhts,bshe->bthe', att, vc)
        b_last = b[:, -1]
        s_new = jnp.exp(b_last)[..., None] * s + jnp.einsum('bshd,bshe->bhde', kc * jnp.exp(b_last[:, None] - b), vc)
        return s_new, o_inter + o_intra

    s_fin, o = lax.scan(step, s0.astype(jnp.float32), (chunks(q), chunks(k), chunks(v), chunks(log_a)))
    return o.swapaxes(0, 1).reshape(B, L, H, DV).astype(v.dtype), s_fin


def rglru(xb, gb, conv_buf, h0, conv_w, conv_b, w_r, b_r, w_i, b_i, lam):
    B, L, W = xb.shape
    xp = jnp.concatenate([conv_buf.astype(xb.dtype), xb], axis=1)
    xc = conv_b + sum(xp[:, j:j + L] * conv_w[j] for j in range(CONV_W))
    xblk = xc.reshape(B, L, RG_BLOCKS, RG_BD)
    r = jax.nn.sigmoid(jnp.einsum('blnd,nde->blne', xblk, w_r).reshape(B, L, W) + b_r)
    i = jax.nn.sigmoid(jnp.einsum('blnd,nde->blne', xblk, w_i).reshape(B, L, W) + b_i)
    log_a = RG_C * r.astype(jnp.float32) * jax.nn.log_sigmoid(lam.astype(jnp.float32))
    a = jnp.exp(log_a)
    u = jnp.sqrt(-jnp.expm1(2.0 * log_a)) * (i * xc).astype(jnp.float32)
    u = u.at[:, 0].add(a[:, 0] * h0.astype(jnp.float32))
    _, h = lax.associative_scan(_linrec, (a, u), axis=1)
    y = (h * jax.nn.gelu(gb.astype(jnp.float32))).astype(xb.dtype)
    return y, xp[:, L:], h[:, -1]


def nsa_core(q, g, kc, vc, ks, vs, kw, vw, w_pos0, cmp_logits):
    lq, lk = q.shape[0], kc.shape[0]
    q_pos0 = lk - lq
    slopes = alibi_slopes(NSA_H)
    n_cmp = lk // CMP_BLOCK
    wc = jax.nn.softmax(cmp_logits.astype(jnp.float32)).astype(kc.dtype)
    kcb = jnp.einsum('njd,j->nd', kc[:n_cmp * CMP_BLOCK].reshape(n_cmp, CMP_BLOCK, NSA_D), wc)
    vcb = jnp.einsum('njd,j->nd', vc[:n_cmp * CMP_BLOCK].reshape(n_cmp, CMP_BLOCK, NSA_D), wc)
    cmp_end = jnp.arange(n_cmp) * CMP_BLOCK + (CMP_BLOCK - 1)
    n_sel = -(-lk // SEL_BLOCK)
    pad = n_sel * SEL_BLOCK - lk
    ksb = jnp.pad(ks, ((0, pad), (0, 0))).reshape(n_sel, SEL_BLOCK, NSA_D)
    vsb = jnp.pad(vs, ((0, pad), (0, 0))).reshape(n_sel, SEL_BLOCK, NSA_D)
    top = min(SEL_TOPK, n_sel)
    sel_ids = jnp.arange(n_sel)
    in_blk = jnp.arange(SEL_BLOCK)
    kwp = jnp.pad(kw, ((WINDOW, 0), (0, 0)))
    vwp = jnp.pad(vw, ((WINDOW, 0), (0, 0)))
    qb = _qblock(lq)

    def block(i):
        q0 = i * qb
        qs = lax.dynamic_slice_in_dim(q, q0, qb, 0) * NSA_SCALE
        gi = jax.nn.sigmoid(lax.dynamic_slice_in_dim(g, q0, qb, 0).astype(jnp.float32))
        t = q_pos0 + q0 + jnp.arange(qb)
        dist_c = (t[:, None] - cmp_end[None, :]).astype(jnp.float32)
        s_c = jnp.einsum('qhd,nd->qhn', qs, kcb).astype(jnp.float32) - slopes[None, :, None] * dist_c[:, None, :]
        p_c = masked_softmax(s_c, (dist_c >= 0)[:, None, :])
        o_c = jnp.einsum('qhn,nd->qhd', p_c, vcb)
        imp = jnp.pad(p_c.sum(1), ((0, 0), (0, n_sel * CMP_PER_SEL - n_cmp)))
        imp = imp.reshape(qb, n_sel, CMP_PER_SEL).sum(-1)
        forced = (sel_ids[None, :] == (t // SEL_BLOCK)[:, None]) | (sel_ids[None, :] == 0)
        causal = sel_ids[None, :] * SEL_BLOCK <= t[:, None]
        imp = jnp.where(forced, jnp.inf, jnp.where(causal, imp, -jnp.inf))
        vals, idx = lax.top_k(imp, top)
        spos = idx[:, :, None] * SEL_BLOCK + in_blk
        dist_s = (t[:, None, None] - spos).astype(jnp.float32)
        s_s = jnp.einsum('qhd,qkjd->qhkj', qs, ksb[idx]).astype(jnp.float32) - slopes[None, :, None, None] * dist_s[:, None]
        m_s = (vals > -jnp.inf)[:, :, None] & (dist_s >= 0)
        p_s = masked_softmax(s_s.reshape(qb, NSA_H, top * SEL_BLOCK), m_s.reshape(qb, 1, top * SEL_BLOCK))
        o_s = jnp.einsum('qhn,qnd->qhd', p_s, vsb[idx].reshape(qb, top * SEL_BLOCK, NSA_D))
        start = q_pos0 + q0 - w_pos0
        kwi = lax.dynamic_slice_in_dim(kwp, start, qb + WINDOW, 0)
        vwi = lax.dynamic_slice_in_dim(vwp, start, qb + WINDOW, 0)
        wpos = q_pos0 + q0 - WINDOW + jnp.arange(qb + WINDOW)
        dist_w = t[:, None] - wpos[None, :]
        s_w = jnp.einsum('qhd,kd->qhk', qs, kwi).astype(jnp.float32) - slopes[None, :, None] * dist_w.astype(jnp.float32)[:, None]
        m_w = (wpos[None, :] >= w_pos0) & (dist_w >= 0) & (dist_w <= WINDOW)
        p_w = masked_softmax(s_w, m_w[:, None])
        o_w = jnp.einsum('qhk,kd->qhd', p_w, vwi)
        return gi[..., 0:1] * o_c + gi[..., 1:2] * o_s + gi[..., 2:3] * o_w

    return lax.map(block, jnp.arange(lq // qb)).reshape(lq, NSA_H, NSA_D)


def mla_core(q_lat, q_rope, c, kr):
    lq, lk = q_lat.shape[0], c.shape[0]
    q_pos0 = lk - lq
    qb = _qblock(lq)
    kpos = jnp.arange(lk)

    def block(i):
        ql = lax.dynamic_slice_in_dim(q_lat, i * qb, qb, 0)
        qr = lax.dynamic_slice_in_dim(q_rope, i * qb, qb, 0)
        t = q_pos0 + i * qb + jnp.arange(qb)
        s = (jnp.einsum('qhr,kr->qhk', ql, c) + jnp.einsum('qhd,kd->qhk', qr, kr)).astype(jnp.float32) * MLA_SCALE
        p = masked_softmax(s, (kpos[None, :] <= t[:, None])[:, None, :])
        return jnp.einsum('qhk,kr->qhr', p, c)

    return lax.map(block, jnp.arange(lq // qb)).reshape(lq, MLA_H, MLA_KVR)


def prompt_attend(nq, ng, nkv, q_lat, q_rope, mla_rows, cmp_logits):
    L = nkv.shape[1]

    def nsa_one(q, g, kv):
        return nsa_core(q, g, kv[:, 0], kv[:, 1], kv[:, 2], kv[:, 3], kv[:, 4], kv[:, 5], 0, cmp_logits)

    o_nsa = jax.vmap(nsa_one)(nq, ng, nkv)
    o_lat = jax.vmap(mla_core)(q_lat, q_rope, mla_rows[..., :MLA_KVR], mla_rows[..., MLA_KVR:])
    win = nkv[:, L - min(WINDOW, L):, 4:]
    return o_nsa, o_lat, win


def make_sample_attend(layer, cache_nsa, win_buf, cache_mla, page_table):
    def attend(nq, ng, nkv, q_lat, q_rope, mla_rows, cmp_logits):
        past = page_table.shape[1] * PAGE_SIZE
        wbuf = win_buf.shape[1]
        keep = min(WINDOW, wbuf + nkv.shape[1])

        def one(args):
            pt, q, g, kv, ql, qr, rows, wb = args
            nsa_past = cache_nsa[layer, pt].reshape(past, 4, NSA_D).astype(kv.dtype)
            full = jnp.concatenate([nsa_past, kv[:, :4]], axis=0)
            wkv = jnp.concatenate([wb.astype(kv.dtype), kv[:, 4:]], axis=0)
            o_n = nsa_core(q, g, full[:, 0], full[:, 1], full[:, 2], full[:, 3], wkv[:, 0], wkv[:, 1], past - wbuf, cmp_logits)
            mla_past = cache_mla[layer, pt].reshape(past, MLA_KVR + MLA_DR).astype(rows.dtype)
            m = jnp.concatenate([mla_past, rows], axis=0)
            o_m = mla_core(ql, qr, m[:, :MLA_KVR], m[:, MLA_KVR:])
            return o_n, o_m, wkv[wkv.shape[0] - keep:]

        return lax.map(one, (page_table, nq, ng, nkv, q_lat, q_rope, mla_rows, win_buf))
    return attend


def trunk_layer(x, pos, attend, s_gla, rg_buf, rg_h, lp):
    (w_in, gla_w_a2, gla_b_a, gla_norm_g, nsa_cmp_logits, rg_conv_w, rg_conv_b, rg_w_r, rg_b_r, rg_w_i,
     rg_b_i, rg_lambda, mla_q_g, mla_w_uq, mla_kv_g, mla_w_uk, mla_w_uv, w_branch, w_out,
     ln1_g, ln1_b, w_up, w_down, ln2_g, ln2_b) = lp
    B, L, _ = x.shape
    z = x @ w_in
    (gq, gk, gv, glr, gr, nq, nkv, ng, rx, rgt, mq, mkv, mkr, mg) = jnp.split(z, IN_SPLITS, axis=-1)
    q = gq.reshape(B, L, GLA_H, GLA_DK) * GLA_DK ** -0.5
    k = gk.reshape(B, L, GLA_H, GLA_DK)
    v = gv.reshape(B, L, GLA_H, GLA_DV)
    log_a = jax.nn.log_sigmoid((glr @ gla_w_a2 + gla_b_a).astype(jnp.float32)).reshape(B, L, GLA_H, GLA_DK) / GLA_TAU
    o_gla, s_gla_new = gla_scan(q, k, v, log_a, s_gla)
    o_gla = rms_norm(o_gla, gla_norm_g) * jax.nn.silu(gr.reshape(B, L, GLA_H, GLA_DV))
    o_rg, rg_buf_new, rg_h_new = rglru(rx, rgt, rg_buf, rg_h, rg_conv_w, rg_conv_b, rg_w_r, rg_b_r, rg_w_i, rg_b_i, rg_lambda)
    nq = nq.reshape(B, L, NSA_H, NSA_D)
    ng = ng.reshape(B, L, NSA_H, 3)
    nkv = nkv.reshape(B, L, 6, NSA_D)
    qm = (rms_norm(mq, mla_q_g) @ mla_w_uq).reshape(B, L, MLA_H, MLA_DN + MLA_DR)
    q_lat = jnp.einsum('blhd,rhd->blhr', qm[..., :MLA_DN], mla_w_uk)
    q_rope = rope(qm[..., MLA_DN:], pos)
    mla_rows = jnp.concatenate([rms_norm(mkv, mla_kv_g), rope(mkr[:, :, None, :], pos)[:, :, 0]], axis=-1)
    o_nsa, o_lat, nsa_win_new = attend(nq, ng, nkv, q_lat, q_rope, mla_rows, nsa_cmp_logits)
    o_mla = jnp.einsum('blhr,rhe->blhe', o_lat, mla_w_uv)
    branches = jnp.stack([o_gla.reshape(B, L, BRANCH_W), o_nsa.reshape(B, L, BRANCH_W),
                          o_rg, o_mla.reshape(B, L, BRANCH_W)], axis=2)
    proj = jnp.einsum('blne,ned->blnd', branches, w_branch)
    merged = jnp.sum(jax.nn.sigmoid(mg.reshape(B, L, N_BRANCH, D_MODEL)) * proj, axis=2)
    x = layer_norm(DN_ALPHA * x + (merged @ w_out).astype(x.dtype), ln1_g, ln1_b)
    h = jnp.square(jax.nn.relu(x @ w_up)) @ w_down
    x = layer_norm(DN_ALPHA * x + h.astype(x.dtype), ln2_g, ln2_b)
    return x, (nkv[:, :, :4], nsa_win_new, mla_rows, s_gla_new, rg_h_new, rg_buf_new)


def _stack(states, i):
    return jnp.stack([s[i] for s in states])


def setup_inputs(seed: int = 0) -> dict:
    key = jax.random.key(seed)
    kit = iter(list(jax.random.split(key, 48)))

    def nrm(shape, scale):
        return scale * jax.random.normal(next(kit), shape, jnp.float32)

    n_pages = PAST_LEN // PAGE_SIZE
    n_used = DEC_BATCH * n_pages
    n_phys = (5 * n_used + 3) // 4
    wbuf = min(WINDOW, PAST_LEN)
    x_prompt = jax.random.normal(next(kit), (BATCH, SEQ, D_MODEL), jnp.float32)
    x_sample = jax.random.normal(next(kit), (DEC_BATCH, DEC_SEQ, D_MODEL), jnp.float32)
    cache_nsa = jax.random.normal(next(kit), (DEPTH, n_phys, PAGE_SIZE, 4, NSA_D), jnp.float32)
    cache_nsa_win = jax.random.normal(next(kit), (DEPTH, DEC_BATCH, wbuf, 2, NSA_D), jnp.float32)
    cache_mla = jax.random.normal(next(kit), (DEPTH, n_phys, PAGE_SIZE, MLA_KVR + MLA_DR), jnp.float32)
    state_gla = nrm((DEPTH, DEC_BATCH, GLA_H, GLA_DK, GLA_DV), 0.5)
    state_rg_h = nrm((DEPTH, DEC_BATCH, RG_W), 0.5)
    state_rg_conv = nrm((DEPTH, DEC_BATCH, CONV_W - 1, RG_W), 1.0)
    page_table = jax.random.permutation(next(kit), n_phys)[:n_used].reshape(DEC_BATCH, n_pages).astype(jnp.int32)
    a0 = jax.random.uniform(next(kit), (DEPTH, RG_W), jnp.float32, 0.9, 0.999)
    s0 = a0 ** (1.0 / RG_C)
    rg_lambda = jnp.log(s0) - jnp.log1p(-s0)
    return {
        'x_prompt': x_prompt, 'x_sample': x_sample,
        'cache_nsa': cache_nsa, 'cache_nsa_win': cache_nsa_win, 'cache_mla': cache_mla,
        'state_gla': state_gla, 'state_rg_h': state_rg_h, 'state_rg_conv': state_rg_conv,
        'page_table': page_table,
        'w_in': nrm((DEPTH, D_MODEL, D_IN), D_MODEL ** -0.5),
        'gla_w_a2': nrm((DEPTH, GLA_RANK, GLA_H * GLA_DK), GLA_RANK ** -0.5),
        'gla_b_a': nrm((DEPTH, GLA_H * GLA_DK), 0.1),
        'gla_norm_g': 1.0 + nrm((DEPTH, GLA_H, GLA_DV), 0.02),
        'nsa_cmp_logits': nrm((DEPTH, CMP_BLOCK), 0.02),
        'rg_conv_w': nrm((DEPTH, CONV_W, RG_W), CONV_W ** -0.5),
        'rg_conv_b': nrm((DEPTH, RG_W), 0.01),
        'rg_w_r': nrm((DEPTH, RG_BLOCKS, RG_BD, RG_BD), RG_BD ** -0.5),
        'rg_b_r': nrm((DEPTH, RG_W), 0.01),
        'rg_w_i': nrm((DEPTH, RG_BLOCKS, RG_BD, RG_BD), RG_BD ** -0.5),
        'rg_b_i': nrm((DEPTH, RG_W), 0.01),
        'rg_lambda': rg_lambda,
        'mla_q_g': 1.0 + nrm((DEPTH, MLA_QR), 0.02),
        'mla_w_uq': nrm((DEPTH, MLA_QR, MLA_H * (MLA_DN + MLA_DR)), MLA_QR ** -0.5),
        'mla_kv_g': 1.0 + nrm((DEPTH, MLA_KVR), 0.02),
        'mla_w_uk': nrm((DEPTH, MLA_KVR, MLA_H, MLA_DN), MLA_KVR ** -0.5),
        'mla_w_uv': nrm((DEPTH, MLA_KVR, MLA_H, MLA_DV), MLA_KVR ** -0.5),
        'w_branch': nrm((DEPTH, N_BRANCH, BRANCH_W, D_MODEL), BRANCH_W ** -0.5 * DN_BETA),
        'w_out': nrm((DEPTH, D_MODEL, D_MODEL), D_MODEL ** -0.5 * DN_BETA),
        'ln1_g': 1.0 + nrm((DEPTH, D_MODEL), 0.02),
        'ln1_b': nrm((DEPTH, D_MODEL), 0.02),
        'w_up': nrm((DEPTH, D_MODEL, D_FF), D_MODEL ** -0.5),
        'w_down': nrm((DEPTH, D_FF, D_MODEL), D_FF ** -0.5 * DN_BETA),
        'ln2_g': 1.0 + nrm((DEPTH, D_MODEL), 0.02),
        'ln2_b': nrm((DEPTH, D_MODEL), 0.02),
    }


def reference(x_prompt, x_sample, cache_nsa, cache_nsa_win, cache_mla, state_gla, state_rg_h, state_rg_conv,
              page_table, w_in, gla_w_a2, gla_b_a, gla_norm_g, nsa_cmp_logits, rg_conv_w, rg_conv_b, rg_w_r,
              rg_b_r, rg_w_i, rg_b_i, rg_lambda, mla_q_g, mla_w_uq, mla_kv_g, mla_w_uk, mla_w_uv, w_branch,
              w_out, ln1_g, ln1_b, w_up, w_down, ln2_g, ln2_b):
    B, S, _ = x_prompt.shape
    DS = x_sample.shape[1]
    past = page_table.shape[1] * PAGE_SIZE
    pos_p = jnp.arange(S, dtype=jnp.int32)
    pos_s = past + jnp.arange(DS, dtype=jnp.int32)
    yp, ys = x_prompt, x_sample
    st_p, st_s = [], []
    for l in range(DEPTH):
        lp = (w_in[l], gla_w_a2[l], gla_b_a[l], gla_norm_g[l], nsa_cmp_logits[l], rg_conv_w[l], rg_conv_b[l],
              rg_w_r[l], rg_b_r[l], rg_w_i[l], rg_b_i[l], rg_lambda[l], mla_q_g[l], mla_w_uq[l], mla_kv_g[l],
              mla_w_uk[l], mla_w_uv[l], w_branch[l], w_out[l], ln1_g[l], ln1_b[l], w_up[l], w_down[l],
              ln2_g[l], ln2_b[l])
        yp, sp = trunk_layer(yp, pos_p, prompt_attend,
                             jnp.zeros((B, GLA_H, GLA_DK, GLA_DV), jnp.float32),
                             jnp.zeros((B, CONV_W - 1, RG_W), yp.dtype),
                             jnp.zeros((B, RG_W), jnp.float32), lp)
        ys, ss = trunk_layer(ys, pos_s, make_sample_attend(l, cache_nsa, cache_nsa_win[l], cache_mla, page_table),
                             state_gla[l], state_rg_conv[l], state_rg_h[l], lp)
        st_p.append(sp)
        st_s.append(ss)
    return (yp, ys,
            _stack(st_p, 0), _stack(st_s, 0),
            _stack(st_p, 1), _stack(st_s, 1),
            _stack(st_p, 2), _stack(st_s, 2),
            _stack(st_p, 3), _stack(st_s, 3),
            _stack(st_p, 4), _stack(st_s, 4),
            _stack(st_p, 5), _stack(st_s, 5))
```

```python
import functools

import numpy as np
import jax
import jax.numpy as jnp
from jax import lax
from jax.experimental import pallas as pl
from jax.experimental.pallas import tpu as pltpu

F32 = jnp.float32
BF16 = jnp.bfloat16
HIGHEST = lax.Precision.HIGHEST

D_MODEL = 1024
DEPTH = 2
PAGE = 128
GLA_H, GLA_DK, GLA_DV, GLA_RANK, GLA_TAU, GLA_CHUNK = 4, 64, 128, 16, 16.0, 64
NSA_H, NSA_D = 8, 64
NSA_SCALE = NSA_D ** -0.5
CMP_BLOCK, SEL_BLOCK, SEL_TOPK, WINDOW, QBLOCK = 32, 64, 16, 512, 128
RG_W, RG_BLOCKS, CONV_W, RG_C = 512, 8, 4, 8.0
RG_BD = RG_W // RG_BLOCKS
MLA_H, MLA_QR, MLA_KVR, MLA_DN, MLA_DR, MLA_DV = 8, 384, 256, 64, 32, 64
MLA_SCALE = (MLA_DN + MLA_DR) ** -0.5
ROPE_BASE = 10000.0
N_BRANCH, BRANCH_W = 4, 512
D_FF = 4 * D_MODEL
DN_ALPHA = (2 * DEPTH) ** 0.25
LN_EPS = 1e-5
IN_SIZES = (GLA_H * GLA_DK, GLA_H * GLA_DK, GLA_H * GLA_DV, GLA_RANK, GLA_H * GLA_DV,
            NSA_H * NSA_D, 6 * NSA_D, 3 * NSA_H, RG_W, RG_W, MLA_QR, MLA_KVR, MLA_DR, N_BRANCH * D_MODEL)
IN_SPLITS = tuple(int(v) for v in np.cumsum(IN_SIZES)[:-1])

LANE = 128
VMEM_LIMIT = 56 * 2 ** 20
NEG = -1e30
MLA_QW = MLA_KVR + LANE

C_MG, C_GV, C_GR, C_NQ, C_RX, C_RGT, C_NKV, C_MQ = 0, 4096, 4608, 5120, 5632, 6144, 6656, 7168
C_GQ, C_GK, C_MKV, C_GLR, C_NG, C_MKR = 7680, 7936, 8192, 8448, 8576, 8704
DZ = 9216


def _cp(sem, vmem=VMEM_LIMIT):
    return pltpu.CompilerParams(dimension_semantics=sem, vmem_limit_bytes=vmem)


def _dot(a, b, precision=None):
    return jnp.dot(a, b, preferred_element_type=F32, precision=precision)


def _dot_nt(a, b):
    return lax.dot_general(a, b, (((1,), (1,)), ((), ())), preferred_element_type=F32)


def _dot_tn(a, b, precision=None):
    return lax.dot_general(a, b, (((0,), (0,)), ((), ())), preferred_element_type=F32, precision=precision)


def _log_sigmoid(x):
    return jnp.minimum(x, 0.0) - jnp.log(1.0 + jnp.exp(-jnp.abs(x)))


def _sigmoid(x):
    return 1.0 / (1.0 + jnp.exp(-x))


def _layer_norm(x, g, b):
    mu = jnp.mean(x, axis=-1, keepdims=True)
    xc = x - mu
    var = jnp.mean(xc * xc, axis=-1, keepdims=True)
    return xc * lax.rsqrt(var + LN_EPS) * g + b


def _alibi_slope(h):
    return lax.bitcast_convert_type((126 - h) << 23, F32)


def _softmax_rows(s, valid):
    s = jnp.where(valid, s, -jnp.inf)
    m = jnp.max(s, axis=-1, keepdims=True)
    m = jnp.where(m > -jnp.inf, m, 0.0)
    e = jnp.exp(s - m)
    d = jnp.sum(e, axis=-1, keepdims=True)
    return e / jnp.where(d > 0, d, 1.0)


def _inproj_body(x_ref, w_ref, o_ref, xb_ref):
    @pl.when(pl.program_id(1) == 0)
    def _():
        xb_ref[...] = x_ref[...].astype(BF16)

    o_ref[...] = _dot(xb_ref[...], w_ref[...])


def in_proj(x, w):
    n = x.shape[0]
    tm, tn = min(n, 1024), 512
    return pl.pallas_call(
        _inproj_body, grid=(n // tm, DZ // tn),
        in_specs=[pl.BlockSpec((tm, D_MODEL), lambda i, j: (i, 0)),
                  pl.BlockSpec((D_MODEL, tn), lambda i, j: (0, j))],
        out_specs=pl.BlockSpec((tm, tn), lambda i, j: (i, j)),
        out_shape=jax.ShapeDtypeStruct((n, DZ), F32),
        scratch_shapes=[pltpu.VMEM((tm, D_MODEL), BF16)],
        compiler_params=_cp(("parallel", "arbitrary")), name="in_proj")(x, w)


def _rope_lanes(x, cos, sa, sb):
    w = x.shape[-1]
    return x * cos + pltpu.roll(x, w - 16, 1) * sa + pltpu.roll(x, 16, 1) * sb


def _prep_body(mq_ref, mkv_ref, mkr_ref, cos_ref, sa_ref, sb_ref, qg_ref, wn_ref, wr_ref, wuk_ref, kvg_ref,
               qm_ref, rows_ref, km_ref):
    mq = mq_ref[...]
    qn = mq * lax.rsqrt(jnp.sum(mq * mq, axis=-1, keepdims=True) * (1.0 / MLA_QR) + 1e-6) * qg_ref[...]
    qb = qn.astype(BF16)
    nope = _dot(qb, wn_ref[...])
    rq = _dot(qb, wr_ref[...])
    qlat = _dot(nope.astype(BF16), wuk_ref[...])
    cos, sa, sb = cos_ref[...], sa_ref[...], sb_ref[...]
    rq = _rope_lanes(rq, jnp.concatenate([cos] * MLA_H, axis=1), jnp.concatenate([sa] * MLA_H, axis=1),
                     jnp.concatenate([sb] * MLA_H, axis=1))
    for h in range(MLA_H):
        qh = jnp.concatenate([qlat[:, h * MLA_KVR:(h + 1) * MLA_KVR], rq[:, h * LANE:(h + 1) * LANE]], axis=1)
        qm_ref[h] = (qh * MLA_SCALE).astype(BF16)
    mkv = mkv_ref[...]
    ckv = mkv * lax.rsqrt(jnp.mean(mkv * mkv, axis=-1, keepdims=True) + 1e-6) * kvg_ref[...]
    kr = _rope_lanes(mkr_ref[...], cos, sa, sb)
    rows_ref[...] = jnp.concatenate([ckv, kr[:, :MLA_DR]], axis=1)
    km_ref[...] = jnp.concatenate([ckv, kr], axis=1).astype(BF16)


def attn_prep(z, tabs, lw, seq_len):
    n = z.shape[0]
    tm = min(n, 512)
    nt = seq_len // tm if seq_len >= tm else 1
    tab_spec = pl.BlockSpec((tm, LANE), lambda i: (i % nt, 0))
    full = lambda a: pl.BlockSpec(a.shape, lambda i: (0,) * a.ndim)
    ws = (lw["mla_q_g"], lw["w_uq_nope"], lw["w_uq_rope"], lw["w_uk_bd"], lw["mla_kv_g"])
    return pl.pallas_call(
        _prep_body, grid=(n // tm,),
        in_specs=[pl.BlockSpec((tm, 512), lambda i: (i, C_MQ // 512)),
                  pl.BlockSpec((tm, 256), lambda i: (i, C_MKV // 256)),
                  pl.BlockSpec((tm, LANE), lambda i: (i, C_MKR // LANE)),
                  tab_spec, tab_spec, tab_spec] + [full(a) for a in ws],
        out_specs=[pl.BlockSpec((MLA_H, tm, MLA_QW), lambda i: (0, i, 0)),
                   pl.BlockSpec((tm, MLA_KVR + MLA_DR), lambda i: (i, 0)),
                   pl.BlockSpec((tm, MLA_QW), lambda i: (i, 0))],
        out_shape=[jax.ShapeDtypeStruct((MLA_H, n, MLA_QW), BF16),
                   jax.ShapeDtypeStruct((n, MLA_KVR + MLA_DR), F32),
                   jax.ShapeDtypeStruct((n, MLA_QW), BF16)],
        compiler_params=_cp(("parallel",)), name="attn_prep")(z, z, z, *tabs, *ws)


def _gla_out(o, gr, ng):
    outs = []
    for h in range(GLA_H):
        oh = o[:, h * GLA_DV:(h + 1) * GLA_DV]
        outs.append(oh * lax.rsqrt(jnp.mean(oh * oh, axis=-1, keepdims=True) + 1e-6))
    y = jnp.concatenate(outs, axis=1) * ng
    return y * (gr * _sigmoid(gr))


def _gla_p_body(q_ref, k_ref, v_ref, gr_ref, glr_ref, wa_ref, ba_ref, ng_ref, o_ref, st_ref, s_ref, *, tile):
    t = pl.program_id(1)
    c = GLA_CHUNK

    @pl.when(t == 0)
    def _():
        s_ref[...] = jnp.zeros_like(s_ref)

    la = _log_sigmoid(_dot(glr_ref[...], wa_ref[...], HIGHEST) + ba_ref[...]) * (1.0 / GLA_TAU)
    row = lax.broadcasted_iota(jnp.int32, (c, c), 0)
    col = lax.broadcasted_iota(jnp.int32, (c, c), 1)
    tri = (row >= col).astype(F32)
    ones = jnp.ones((c, GLA_DV), F32)
    for ci in range(tile // c):
        rows = slice(ci * c, (ci + 1) * c)
        g = la[rows]
        b = _dot(tri, g, HIGHEST)
        blast_rep = _dot_tn(g, ones, HIGHEST)
        q = q_ref[rows, :] * (GLA_DK ** -0.5)
        k = k_ref[rows, :]
        v = v_ref[rows, :]
        outs = []
        for h in range(GLA_H):
            dk = slice(h * GLA_DK, (h + 1) * GLA_DK)
            bh, qh, kh = b[:, dk], q[:, dk], k[:, dk]
            vh = v[:, h * GLA_DV:(h + 1) * GLA_DV].astype(BF16)
            bmid = bh[c // 2 - 1:c // 2]
            blast = bh[c - 1:c]
            s_old = s_ref[h]
            o_inter = _dot((qh * jnp.exp(bh)).astype(BF16), s_old.astype(BF16))
            att = _dot_nt((qh * jnp.exp(bh - bmid)).astype(BF16), (kh * jnp.exp(bmid - bh)).astype(BF16)) * tri
            o_intra = _dot(att.astype(BF16), vh)
            kl = (kh * jnp.exp(blast - bh)).astype(BF16)
            s_ref[h] = jnp.exp(blast_rep[dk]) * s_old + _dot_tn(kl, vh)
            outs.append(o_inter + o_intra)
        o_ref[rows, :] = _gla_out(jnp.concatenate(outs, axis=1), gr_ref[rows, :], ng_ref[...])

    @pl.when(t == pl.num_programs(1) - 1)
    def _():
        st_ref[0] = s_ref[...]


def gla_prompt(z, lw, batch, seq_len):
    tile = 256
    nt = seq_len // tile
    rb = lambda b, t: b * nt + t
    full = lambda a: pl.BlockSpec(a.shape, lambda b, t: (0,) * a.ndim)
    ws = (lw["gla_w_a2p"], lw["gla_b_a"], lw["gla_norm_g"])
    return pl.pallas_call(
        functools.partial(_gla_p_body, tile=tile), grid=(batch, nt),
        in_specs=[pl.BlockSpec((tile, 256), lambda b, t: (rb(b, t), C_GQ // 256)),
                  pl.BlockSpec((tile, 256), lambda b, t: (rb(b, t), C_GK // 256)),
                  pl.BlockSpec((tile, 512), lambda b, t: (rb(b, t), C_GV // 512)),
                  pl.BlockSpec((tile, 512), lambda b, t: (rb(b, t), C_GR // 512)),
                  pl.BlockSpec((tile, LANE), lambda b, t: (rb(b, t), C_GLR // LANE))] + [full(a) for a in ws],
        out_specs=[pl.BlockSpec((tile, 512), lambda b, t: (rb(b, t), 0)),
                   pl.BlockSpec((1, GLA_H, GLA_DK, GLA_DV), lambda b, t: (b, 0, 0, 0))],
        out_shape=[jax.ShapeDtypeStruct((batch * seq_len, 512), F32),
                   jax.ShapeDtypeStruct((batch, GLA_H, GLA_DK, GLA_DV), F32)],
        scratch_shapes=[pltpu.VMEM((GLA_H, GLA_DK, GLA_DV), F32)],
        compiler_params=_cp(("arbitrary", "arbitrary")), name="gla_prompt")(z, z, z, z, z, *ws)


def _gla_s_body(q_ref, k_ref, v_ref, gr_ref, glr_ref, wa_ref, ba_ref, ng_ref, st_ref, o_ref, sto_ref):
    la = _log_sigmoid(_dot(glr_ref[...], wa_ref[0], HIGHEST) + ba_ref[0]) * (1.0 / GLA_TAU)
    ea = jnp.exp(la)
    q = q_ref[0] * (GLA_DK ** -0.5)
    k = k_ref[0]
    v = v_ref[...]
    o = jnp.zeros(v.shape, F32)
    for d in range(GLA_DK):
        s_new = ea[:, d:d + 1] * st_ref[:, d, :] + k[:, d:d + 1] * v
        sto_ref[:, d, :] = s_new
        o = o + q[:, d:d + 1] * s_new
    y = o * lax.rsqrt(jnp.mean(o * o, axis=-1, keepdims=True) + 1e-6) * ng_ref[0]
    gr = gr_ref[...]
    o_ref[...] = y * (gr * _sigmoid(gr))


def gla_decode(z, state, lw):
    s = z.shape[0]
    heads = lambda a: a.reshape(s, GLA_H, GLA_DK).transpose(1, 0, 2)
    qh = heads(z[:, C_GQ:C_GQ + 256])
    kh = heads(z[:, C_GK:C_GK + 256])
    st = state.reshape(s, GLA_H * GLA_DK, GLA_DV)
    o, st_new = pl.pallas_call(
        _gla_s_body, grid=(GLA_H,),
        in_specs=[pl.BlockSpec((1, s, GLA_DK), lambda h: (h, 0, 0)),
                  pl.BlockSpec((1, s, GLA_DK), lambda h: (h, 0, 0)),
                  pl.BlockSpec((s, GLA_DV), lambda h: (0, C_GV // GLA_DV + h)),
                  pl.BlockSpec((s, GLA_DV), lambda h: (0, C_GR // GLA_DV + h)),
                  pl.BlockSpec((s, LANE), lambda h: (0, C_GLR // LANE)),
                  pl.BlockSpec((1, LANE, GLA_DK), lambda h: (h, 0, 0)),
                  pl.BlockSpec((1, 1, GLA_DK), lambda h: (h, 0, 0)),
                  pl.BlockSpec((1, 1, GLA_DV), lambda h: (h, 0, 0)),
                  pl.BlockSpec((s, GLA_DK, GLA_DV), lambda h: (0, h, 0))],
        out_specs=[pl.BlockSpec((s, GLA_DV), lambda h: (0, h)),
                   pl.BlockSpec((s, GLA_DK, GLA_DV), lambda h: (0, h, 0))],
        out_shape=[jax.ShapeDtypeStruct((s, GLA_H * GLA_DV), F32),
                   jax.ShapeDtypeStruct(st.shape, F32)],
        compiler_params=_cp(("arbitrary",)), name="gla_decode")(
            qh, kh, z, z, z, lw["gla_w_a2h"], lw["gla_b_ah"], lw["gla_norm_gh"], st)
    return o, st_new.reshape(state.shape)


def _rg_gates(xc, wr_ref, br_ref, wi_ref, bi_ref, lam_ref):
    xb = xc.astype(BF16)
    r = _sigmoid(_dot(xb, wr_ref[...]) + br_ref[...])
    i = _sigmoid(_dot(xb, wi_ref[...]) + bi_ref[...])
    log_a = RG_C * r * _log_sigmoid(lam_ref[...])
    a = jnp.exp(log_a)
    u = jnp.sqrt(1.0 - jnp.exp(2.0 * log_a)) * (i * xc)
    return a, u


def _gelu_tanh(x):
    return 0.5 * x * (1.0 + jnp.tanh(0.7978845608028654 * (x + 0.044715 * x * x * x)))


def _rg_p_body(x_ref, g_ref, cw_ref, cb_ref, wr_ref, br_ref, wi_ref, bi_ref, lam_ref,
               y_ref, h_ref, buf_ref, xe_ref, hc_ref, *, tile):
    t = pl.program_id(1)

    @pl.when(t == 0)
    def _():
        xe_ref[0:8, :] = jnp.zeros((8, RG_W), F32)
        hc_ref[...] = jnp.zeros_like(hc_ref)

    x = x_ref[...]
    xe_ref[8:8 + tile, :] = x
    cw = cw_ref[...]
    xc = cb_ref[...] + x * cw[3:4]
    for j in range(1, CONV_W):
        xc = xc + xe_ref[8 - j:8 - j + tile, :] * cw[3 - j:4 - j]
    a, u = _rg_gates(xc, wr_ref, br_ref, wi_ref, bi_ref, lam_ref)
    row = lax.broadcasted_iota(jnp.int32, (tile, RG_W), 0)
    s = 1
    while s < tile:
        keep = row >= s
        a_prev = jnp.where(keep, pltpu.roll(a, s, 0), 1.0)
        u_prev = jnp.where(keep, pltpu.roll(u, s, 0), 0.0)
        u = a * u_prev + u
        a = a * a_prev
        s *= 2
    h = u + a * hc_ref[...]
    hc_ref[...] = h[tile - 1:tile]
    xe_ref[0:8, :] = x[tile - 8:tile]
    y_ref[...] = h * _gelu_tanh(g_ref[...])

    @pl.when(t == pl.num_programs(1) - 1)
    def _():
        h_ref[0] = h[tile - 1:tile]
        buf_ref[0] = x[tile - (CONV_W - 1):tile]


def rglru_prompt(z, lw, batch, seq_len):
    tile = 256
    nt = seq_len // tile
    rb = lambda b, t: b * nt + t
    full = lambda a: pl.BlockSpec(a.shape, lambda b, t: (0,) * a.ndim)
    ws = (lw["rg_conv_w"], lw["rg_conv_b"], lw["rg_w_r_bd"], lw["rg_b_r"], lw["rg_w_i_bd"], lw["rg_b_i"],
          lw["rg_lambda"])
    return pl.pallas_call(
        functools.partial(_rg_p_body, tile=tile), grid=(batch, nt),
        in_specs=[pl.BlockSpec((tile, RG_W), lambda b, t: (rb(b, t), C_RX // RG_W)),
                  pl.BlockSpec((tile, RG_W), lambda b, t: (rb(b, t), C_RGT // RG_W))] + [full(a) for a in ws],
        out_specs=[pl.BlockSpec((tile, RG_W), lambda b, t: (rb(b, t), 0)),
                   pl.BlockSpec((1, 1, RG_W), lambda b, t: (b, 0, 0)),
                   pl.BlockSpec((1, CONV_W - 1, RG_W), lambda b, t: (b, 0, 0))],
        out_shape=[jax.ShapeDtypeStruct((batch * seq_len, RG_W), F32),
                   jax.ShapeDtypeStruct((batch, 1, RG_W), F32),
                   jax.ShapeDtypeStruct((batch, CONV_W - 1, RG_W), F32)],
        scratch_shapes=[pltpu.VMEM((8 + tile, RG_W), F32), pltpu.VMEM((1, RG_W), F32)],
        compiler_params=_cp(("arbitrary", "arbitrary")), name="rglru_prompt")(z, z, *ws)


def _rg_s_body(x_ref, g_ref, b0_ref, b1_ref, b2_ref, h0_ref, cw_ref, cb_ref, wr_ref, br_ref, wi_ref, bi_ref,
               lam_ref, y_ref, h_ref):
    cw = cw_ref[...]
    x = x_ref[...]
    xc = cb_ref[...] + b0_ref[...] * cw[0:1] + b1_ref[...] * cw[1:2] + b2_ref[...] * cw[2:3] + x * cw[3:4]
    a, u = _rg_gates(xc, wr_ref, br_ref, wi_ref, bi_ref, lam_ref)
    h = u + a * h0_ref[...]
    h_ref[...] = h
    y_ref[...] = h * _gelu_tanh(g_ref[...])


def rglru_decode(z, conv_buf, h0, lw):
    s = z.shape[0]
    buf = conv_buf.reshape(s, (CONV_W - 1) * RG_W)
    full = lambda a: pl.BlockSpec(a.shape, lambda i: (0,) * a.ndim)
    ws = (lw["rg_conv_w"], lw["rg_conv_b"], lw["rg_w_r_bd"], lw["rg_b_r"], lw["rg_w_i_bd"], lw["rg_b_i"],
          lw["rg_lambda"])
    blk = lambda c: pl.BlockSpec((s, RG_W), lambda i: (0, c))
    y, h = pl.pallas_call(
        _rg_s_body, grid=(1,),
        in_specs=[blk(C_RX // RG_W), blk(C_RGT // RG_W), blk(0), blk(1), blk(2), blk(0)] + [full(a) for a in ws],
        out_specs=[blk(0), blk(0)],
        out_shape=[jax.ShapeDtypeStruct((s, RG_W), F32), jax.ShapeDtypeStruct((s, RG_W), F32)],
        compiler_params=_cp(("arbitrary",)), name="rglru_decode")(z, z, buf, buf, buf, h0, *ws)
    buf_new = jnp.concatenate([conv_buf[:, 1:], z[:, None, C_RX:C_RX + RG_W]], axis=1)
    return y, buf_new, h


def _cmp_weights(lg):
    e = jnp.exp(lg - jnp.max(lg, axis=-1, keepdims=True))
    return e / (jnp.sum(e, axis=-1, keepdims=True) * (CMP_BLOCK / lg.shape[-1]))


def _topk_mask(vals, lane, k):
    sel = jnp.zeros(vals.shape, jnp.bool_)
    idxs, oks = [], []
    lane = lane.astype(F32)
    big = float(vals.shape[-1])
    for _ in range(k):
        m = jnp.max(vals, axis=-1, keepdims=True)
        idx = jnp.min(jnp.where(vals == m, lane, big), axis=-1, keepdims=True)
        pick = lane == idx
        ok = m > -jnp.inf
        sel = sel | (pick & ok)
        vals = jnp.where(pick, -jnp.inf, vals)
        idxs.append(idx)
        oks.append(ok)
    return sel, idxs, oks


def _nsa_p_body(q_ref, g_ref, kvc_ref, kvs_ref, kvw_ref, lg_ref, o_ref,
                kcb_ref, vcb_ref, ks_ref, vs_ref, kw_ref, vw_ref, *, seq_len):
    i = pl.program_id(1)
    qb = QBLOCK
    n_cmp = seq_len // CMP_BLOCK
    tk = 512
    rows = NSA_H * qb

    @pl.when(i == 0)
    def _():
        wc = _cmp_weights(lg_ref[...])
        blk = lax.broadcasted_iota(jnp.int32, (n_cmp, seq_len), 0)
        pos = lax.broadcasted_iota(jnp.int32, (n_cmp, seq_len), 1)
        wmat = jnp.where((pos // CMP_BLOCK) == blk, wc, 0.0).astype(BF16)
        kvcb = _dot(wmat, kvc_ref[...].astype(BF16))
        kcb_ref[...] = kvcb[:, :NSA_D].astype(BF16)
        vcb_ref[...] = kvcb[:, NSA_D:].astype(BF16)
        kvs = kvs_ref[...]
        ks_ref[...] = kvs[:, :NSA_D].astype(BF16)
        vs_ref[...] = kvs[:, NSA_D:].astype(BF16)
        kvw = kvw_ref[...]
        kw_ref[...] = kvw[:, :NSA_D].astype(BF16)
        vw_ref[...] = kvw[:, NSA_D:].astype(BF16)

    q = q_ref[...] * NSA_SCALE
    q2 = jnp.concatenate([q[:, h * NSA_D:(h + 1) * NSA_D] for h in range(NSA_H)], axis=0).astype(BF16)
    t0 = i * qb
    tq = t0 + lax.broadcasted_iota(jnp.int32, (qb, 1), 0)
    slope3 = _alibi_slope(lax.broadcasted_iota(jnp.int32, (NSA_H, 1, 1), 0))

    lane_c = lax.broadcasted_iota(jnp.int32, (1, n_cmp), 1)
    dist_c = (tq - (lane_c * CMP_BLOCK + (CMP_BLOCK - 1))).astype(F32)
    s_c = _dot_nt(q2, kcb_ref[...]).reshape(NSA_H, qb, n_cmp) - slope3 * dist_c[None]
    p_c = _softmax_rows(s_c, (dist_c >= 0)[None])
    o_c = _dot(p_c.reshape(rows, n_cmp).astype(BF16), vcb_ref[...])

    imp = jnp.sum(p_c, axis=0)
    n_sel = seq_len // SEL_BLOCK
    pr = lax.broadcasted_iota(jnp.int32, (n_cmp, n_cmp), 0)
    pc = lax.broadcasted_iota(jnp.int32, (n_cmp, n_cmp), 1)
    pair = ((pr // (SEL_BLOCK // CMP_BLOCK)) == pc).astype(F32)
    imp = _dot(imp, pair, HIGHEST)
    forced = (lane_c == tq // SEL_BLOCK) | (lane_c == 0)
    causal = (lane_c * SEL_BLOCK <= tq) & (lane_c < n_sel)
    imp = jnp.where(forced, jnp.inf, jnp.where(causal, imp, -jnp.inf))
    sel, _, _ = _topk_mask(imp, lane_c, min(SEL_TOPK, n_sel))
    sel_b = sel.astype(BF16)

    def sel_tile(j, carry):
        m, l, acc = carry
        k0 = pl.multiple_of(j * tk, tk)
        kt = ks_ref[pl.ds(k0, tk), :]
        vt = vs_ref[pl.ds(k0, tk), :]
        kpos = k0 + lax.broadcasted_iota(jnp.int32, (1, tk), 1)
        er = lax.broadcasted_iota(jnp.int32, (n_cmp, tk), 0)
        ec = k0 + lax.broadcasted_iota(jnp.int32, (n_cmp, tk), 1)
        expand = ((ec // SEL_BLOCK) == er).astype(BF16)
        ok = (_dot(sel_b, expand) > 0.5) & (kpos <= tq)
        dist = (tq - kpos).astype(F32)
        bias = jnp.where(ok, 0.0, NEG)
        s = _dot_nt(q2, kt).reshape(NSA_H, qb, tk) - slope3 * dist[None] + bias[None]
        s = s.reshape(rows, tk)
        m_new = jnp.maximum(m, jnp.max(s, axis=-1, keepdims=True))
        alpha = jnp.exp(m - m_new)
        p = jnp.exp(s - m_new)
        l = alpha * l + jnp.sum(p, axis=-1, keepdims=True)
        acc = alpha * acc + _dot(p.astype(BF16), vt)
        return m_new, l, acc

    n_tiles = (t0 + qb + tk - 1) // tk
    m0 = jnp.full((rows, 1), NEG, F32)
    _, l_s, acc_s = lax.fori_loop(0, n_tiles, sel_tile, (m0, jnp.zeros((rows, 1), F32), jnp.zeros((rows, NSA_D), F32)))
    o_s = acc_s / l_s

    wk = min(WINDOW + qb, seq_len)
    w0 = pl.multiple_of(jnp.maximum(t0 - WINDOW, 0), qb)
    wpos = w0 + lax.broadcasted_iota(jnp.int32, (1, wk), 1)
    dist_w = tq - wpos
    ok_w = (dist_w >= 0) & (dist_w <= WINDOW)
    s_w = _dot_nt(q2, kw_ref[pl.ds(w0, wk), :]).reshape(NSA_H, qb, wk) - slope3 * dist_w.astype(F32)[None]
    p_w = _softmax_rows(s_w, ok_w[None])
    o_w = _dot(p_w.reshape(rows, wk).astype(BF16), vw_ref[pl.ds(w0, wk), :])

    gi = _sigmoid(g_ref[...])
    outs = []
    for h in range(NSA_H):
        r = slice(h * qb, (h + 1) * qb)
        outs.append(gi[:, 3 * h:3 * h + 1] * o_c[r] + gi[:, 3 * h + 1:3 * h + 2] * o_s[r]
                    + gi[:, 3 * h + 2:3 * h + 3] * o_w[r])
    o_ref[...] = jnp.concatenate(outs, axis=1)


def nsa_prompt(z, lw, batch, seq_len):
    nq = seq_len // QBLOCK
    n_cmp = seq_len // CMP_BLOCK
    rb = lambda b, i: b * nq + i
    kv = lambda c: pl.BlockSpec((seq_len, LANE), lambda b, i: (b, c))
    lg = jnp.tile(lw["nsa_cmp_logits"], seq_len // CMP_BLOCK)[None]
    return pl.pallas_call(
        functools.partial(_nsa_p_body, seq_len=seq_len), grid=(batch, nq),
        in_specs=[pl.BlockSpec((QBLOCK, 512), lambda b, i: (rb(b, i), C_NQ // 512)),
                  pl.BlockSpec((QBLOCK, LANE), lambda b, i: (rb(b, i), C_NG // LANE)),
                  kv(C_NKV // LANE), kv(C_NKV // LANE + 1), kv(C_NKV // LANE + 2),
                  pl.BlockSpec((1, seq_len), lambda b, i: (0, 0))],
        out_specs=pl.BlockSpec((QBLOCK, 512), lambda b, i: (rb(b, i), 0)),
        out_shape=jax.ShapeDtypeStruct((batch * seq_len, 512), F32),
        scratch_shapes=[pltpu.VMEM((n_cmp, NSA_D), BF16), pltpu.VMEM((n_cmp, NSA_D), BF16)]
        + [pltpu.VMEM((seq_len, NSA_D), BF16)] * 4,
        compiler_params=_cp(("arbitrary", "arbitrary")), name="nsa_prompt")(z, z, z, z, z, lg)


def _mla_p_body(q_ref, k_ref, o_ref):
    i = pl.program_id(1)
    qb, tk = QBLOCK, 512
    rows = MLA_H * qb
    q2 = q_ref[...].reshape(rows, MLA_QW)
    tq = i * qb + (lax.broadcasted_iota(jnp.int32, (rows, 1), 0) & (qb - 1))

    def tile(j, carry):
        m, l, acc = carry
        k0 = pl.multiple_of(j * tk, tk)
        kt = k_ref[pl.ds(k0, tk), :]
        kpos = k0 + lax.broadcasted_iota(jnp.int32, (1, tk), 1)
        s = jnp.where(kpos <= tq, _dot_nt(q2, kt), NEG)
        m_new = jnp.maximum(m, jnp.max(s, axis=-1, keepdims=True))
        alpha = jnp.exp(m - m_new)
        p = jnp.exp(s - m_new)
        l = alpha * l + jnp.sum(p, axis=-1, keepdims=True)
        acc = alpha * acc + _dot(p.astype(BF16), kt[:, :MLA_KVR])
        return m_new, l, acc

    n_tiles = (i * qb + qb + tk - 1) // tk
    m0 = jnp.full((rows, 1), NEG, F32)
    _, l, acc = lax.fori_loop(0, n_tiles, tile, (m0, jnp.zeros((rows, 1), F32), jnp.zeros((rows, MLA_KVR), F32)))
    o = (acc / l).astype(BF16)
    for h in range(MLA_H):
        o_ref[:, h * MLA_KVR:(h + 1) * MLA_KVR] = o[h * qb:(h + 1) * qb]


def mla_prompt(qm, km, batch, seq_len):
    nq = seq_len // QBLOCK
    return pl.pallas_call(
        _mla_p_body, grid=(batch, nq),
        in_specs=[pl.BlockSpec((MLA_H, QBLOCK, MLA_QW), lambda b, i: (0, b * nq + i, 0)),
                  pl.BlockSpec((seq_len, MLA_QW), lambda b, i: (b, 0))],
        out_specs=pl.BlockSpec((QBLOCK, MLA_H * MLA_KVR), lambda b, i: (b * nq + i, 0)),
        out_shape=jax.ShapeDtypeStruct((batch * seq_len, MLA_H * MLA_KVR), BF16),
        compiler_params=_cp(("arbitrary", "arbitrary")), name="mla_prompt")(qm, km)


CH_PAGES = 16
CH_ROWS = CH_PAGES * PAGE
HPAD = 16


def _chunk_copies(cache_ref, layer, pt_ref, buf_ref, sem_ref, seq, chunk, slot, lanes):
    copies = []
    for p in range(CH_PAGES):
        page = pt_ref[seq, chunk * CH_PAGES + p]
        src = cache_ref.at[layer, page] if lanes is None else cache_ref.at[layer, page, :, pl.ds(lanes[0], lanes[1])]
        copies.append(pltpu.make_async_copy(src, buf_ref.at[slot, pl.ds(p * PAGE, PAGE), :], sem_ref.at[slot]))
    return copies


def _paged_pipeline(cache_ref, layer, pt_ref, buf_ref, sem_ref, n_chunks, lanes, compute):
    s = pl.program_id(0)
    n_seq = pl.num_programs(0)

    @pl.when(s == 0)
    def _():
        for cp in _chunk_copies(cache_ref, layer, pt_ref, buf_ref, sem_ref, 0, 0, 0, lanes):
            cp.start()

    def body(c, carry):
        slot = c % 2
        nxt = c + 1

        @pl.when(nxt < n_chunks)
        def _():
            for cp in _chunk_copies(cache_ref, layer, pt_ref, buf_ref, sem_ref, s, nxt, 1 - slot, lanes):
                cp.start()

        @pl.when((nxt == n_chunks) & (s + 1 < n_seq))
        def _():
            for cp in _chunk_copies(cache_ref, layer, pt_ref, buf_ref, sem_ref, s + 1, 0, 1 - slot, lanes):
                cp.start()

        for cp in _chunk_copies(cache_ref, layer, pt_ref, buf_ref, sem_ref, s, c, slot, lanes):
            cp.wait()
        return compute(c, slot, carry)

    return body


def _nsa_cmp_body(pt_ref, q_ref, lg_ref, cache_ref, oc_ref, imp_ref, buf_ref, sem_ref, wm_ref, kv_ref,
                  *, layer, n_chunks, t_pos):
    s = pl.program_id(0)
    cmp_rows = CH_ROWS // CMP_BLOCK

    @pl.when(s == 0)
    def _():
        wc = _cmp_weights(lg_ref[...])
        blk = lax.broadcasted_iota(jnp.int32, (cmp_rows, CH_ROWS), 0)
        pos = lax.broadcasted_iota(jnp.int32, (cmp_rows, CH_ROWS), 1)
        wm_ref[...] = jnp.where((pos // CMP_BLOCK) == blk, wc, 0.0).astype(BF16)

    def compute(c, slot, carry):
        r0 = pl.multiple_of(c * cmp_rows, cmp_rows)
        kv_ref[pl.ds(r0, cmp_rows), :] = _dot(wm_ref[...], buf_ref[slot].astype(BF16))
        return carry

    body = _paged_pipeline(cache_ref, layer, pt_ref, buf_ref, sem_ref, n_chunks, (0, LANE), compute)
    lax.fori_loop(0, n_chunks, body, 0)

    n_cmp = n_chunks * cmp_rows
    kvcb = kv_ref[...]
    q2 = (q_ref[0] * NSA_SCALE).astype(BF16)
    hrow = lax.broadcasted_iota(jnp.int32, (HPAD, 1), 0)
    slope = _alibi_slope(jnp.minimum(hrow, NSA_H - 1))
    lane_c = lax.broadcasted_iota(jnp.int32, (1, n_cmp), 1)
    dist_c = (t_pos - (lane_c * CMP_BLOCK + (CMP_BLOCK - 1))).astype(F32)
    s_c = _dot_nt(q2, kvcb[:, :NSA_D].astype(BF16)) - slope * dist_c
    p_c = _softmax_rows(s_c, dist_c >= 0)
    oc_ref[0] = _dot(p_c.astype(BF16), kvcb[:, NSA_D:].astype(BF16))
    imp_ref[0] = jnp.sum(jnp.where(hrow < NSA_H, p_c, 0.0), axis=0, keepdims=True)


def nsa_decode_cmp(q16, page_table, cache4, lw, layer):
    n_seq, n_pages = page_table.shape
    n_chunks = n_pages // CH_PAGES
    n_cmp = n_pages * PAGE // CMP_BLOCK
    lg = jnp.tile(lw["nsa_cmp_logits"], CH_ROWS // CMP_BLOCK)[None]
    gs = pltpu.PrefetchScalarGridSpec(
        num_scalar_prefetch=1, grid=(n_seq,),
        in_specs=[pl.BlockSpec((1, HPAD, NSA_D), lambda s, pt: (s, 0, 0)),
                  pl.BlockSpec((1, CH_ROWS), lambda s, pt: (0, 0)),
                  pl.BlockSpec(memory_space=pl.ANY)],
        out_specs=[pl.BlockSpec((1, HPAD, NSA_D), lambda s, pt: (s, 0, 0)),
                   pl.BlockSpec((1, 1, n_cmp), lambda s, pt: (s, 0, 0))],
        scratch_shapes=[pltpu.VMEM((2, CH_ROWS, LANE), F32), pltpu.SemaphoreType.DMA((2,)),
                        pltpu.VMEM((CH_ROWS // CMP_BLOCK, CH_ROWS), BF16), pltpu.VMEM((n_cmp, LANE), F32)])
    return pl.pallas_call(
        functools.partial(_nsa_cmp_body, layer=layer, n_chunks=n_chunks, t_pos=n_pages * PAGE),
        grid_spec=gs,
        out_shape=[jax.ShapeDtypeStruct((n_seq, HPAD, NSA_D), F32), jax.ShapeDtypeStruct((n_seq, 1, n_cmp), F32)],
        compiler_params=_cp(("arbitrary",)), name="nsa_decode_cmp")(page_table, q16, lg, cache4)


def _nsa_topk_body(imp_ref, idx_ref, *, t_pos):
    imp = imp_ref[...]
    n_seq, n_cmp = imp.shape
    n_sel = -(-(t_pos + 1) // SEL_BLOCK)
    width = idx_ref.shape[-1]
    pr = lax.broadcasted_iota(jnp.int32, (n_cmp, width), 0)
    pc = lax.broadcasted_iota(jnp.int32, (n_cmp, width), 1)
    pair = ((pr // (SEL_BLOCK // CMP_BLOCK)) == pc).astype(F32)
    vals = _dot(imp, pair, HIGHEST)
    lane = lax.broadcasted_iota(jnp.int32, (1, width), 1)
    forced = (lane == t_pos // SEL_BLOCK) | (lane == 0)
    causal = (lane * SEL_BLOCK <= t_pos) & (lane < n_sel)
    vals = jnp.where(forced, jnp.inf, jnp.where(causal, vals, -jnp.inf))
    _, idxs, oks = _topk_mask(vals, lane, min(SEL_TOPK, n_sel))
    out = jnp.full((n_seq, width), -1, jnp.int32)
    for r, (idx, ok) in enumerate(zip(idxs, oks)):
        out = jnp.where(lane == r, jnp.where(ok, idx.astype(jnp.int32), -1), out)
    idx_ref[...] = out


def nsa_decode_topk(imp, t_pos):
    n_seq, n_cmp = imp.shape
    width = 384
    return pl.pallas_call(
        functools.partial(_nsa_topk_body, t_pos=t_pos), grid=(1,),
        in_specs=[pl.BlockSpec((n_seq, n_cmp), lambda i: (0, 0))],
        out_specs=pl.BlockSpec((n_seq, width), lambda i: (0, 0)),
        out_shape=jax.ShapeDtypeStruct((n_seq, width), jnp.int32),
        compiler_params=_cp(("arbitrary",)), name="nsa_decode_topk")(imp)


def _sel_copies(cache_ref, layer, pt_ref, idx_ref, buf_ref, sem_ref, seq, slot, n_blk):
    per_page = PAGE // SEL_BLOCK
    copies = []
    for r in range(SEL_TOPK):
        blk = jnp.clip(idx_ref[seq, r], 0, n_blk - 1)
        page = pt_ref[seq, blk // per_page]
        row0 = pl.multiple_of((blk % per_page) * SEL_BLOCK, SEL_BLOCK)
        src = cache_ref.at[layer, page, pl.ds(row0, SEL_BLOCK), pl.ds(LANE, LANE)]
        copies.append(pltpu.make_async_copy(src, buf_ref.at[slot, pl.ds(r * SEL_BLOCK, SEL_BLOCK), :],
                                            sem_ref.at[slot]))
    return copies


def _attend_with_new(q2, slope, k_b, v_b, dist, ok, q_f32, k_new, v_new):
    s = jnp.where(ok, _dot_nt(q2, k_b) - slope * dist, NEG)
    s_new = jnp.sum(q_f32 * k_new, axis=-1, keepdims=True)
    m = jnp.maximum(jnp.max(s, axis=-1, keepdims=True), s_new)
    e = jnp.exp(s - m)
    e_new = jnp.exp(s_new - m)
    d = jnp.sum(e, axis=-1, keepdims=True) + e_new
    return (_dot(e.astype(BF16), v_b) + e_new * v_new) / d


def _nsa_sel_body(pt_ref, idx_ref, q_ref, g_ref, oc_ref, new_ref, win_ref, cache_ref, o_ref, buf_ref, sem_ref,
                  *, layer, t_pos):
    s = pl.program_id(0)
    n_seq = pl.num_programs(0)
    n_blk = t_pos // SEL_BLOCK
    slot = s % 2
    args = (cache_ref, layer, pt_ref, idx_ref, buf_ref, sem_ref)

    @pl.when(s == 0)
    def _():
        for cp in _sel_copies(*args, 0, 0, n_blk):
            cp.start()

    @pl.when(s + 1 < n_seq)
    def _():
        for cp in _sel_copies(*args, s + 1, 1 - slot, n_blk):
            cp.start()

    for cp in _sel_copies(*args, s, slot, n_blk):
        cp.wait()

    q = q_ref[0] * NSA_SCALE
    q2 = q.astype(BF16)
    hrow = lax.broadcasted_iota(jnp.int32, (HPAD, 1), 0)
    slope = _alibi_slope(jnp.minimum(hrow, NSA_H - 1))
    new = new_ref[0]
    nk = SEL_TOPK * SEL_BLOCK
    lane = lax.broadcasted_iota(jnp.int32, (1, nk), 1)
    blk_of = jnp.zeros((1, nk), jnp.int32)
    for r in range(SEL_TOPK):
        blk_of = jnp.where(lane // SEL_BLOCK == r, idx_ref[s, r], blk_of)
    pos = blk_of * SEL_BLOCK + (lane % SEL_BLOCK)
    ok = (blk_of >= 0) & (blk_of < n_blk)
    kv = buf_ref[slot]
    o_s = _attend_with_new(q2, slope, kv[:, :NSA_D].astype(BF16), kv[:, NSA_D:].astype(BF16),
                           (t_pos - pos).astype(F32), ok, q, new[:, 2 * NSA_D:3 * NSA_D], new[:, 3 * NSA_D:4 * NSA_D])
    wkv = win_ref[0]
    wbuf = wkv.shape[0]
    lane_w = lax.broadcasted_iota(jnp.int32, (1, wbuf), 1)
    dist_w = wbuf - lane_w
    ok_w = dist_w <= WINDOW
    o_w = _attend_with_new(q2, slope, wkv[:, :NSA_D].astype(BF16), wkv[:, NSA_D:].astype(BF16),
                           dist_w.astype(F32), ok_w, q, new[:, 4 * NSA_D:5 * NSA_D], new[:, 5 * NSA_D:6 * NSA_D])
    gi = _sigmoid(g_ref[0])
    o_ref[0] = gi[:, 0:1] * oc_ref[0] + gi[:, 1:2] * o_s + gi[:, 2:3] * o_w


def nsa_decode_sel(q16, g16, o_c, new_rows, win, page_table, idx, cache4, layer):
    n_seq, n_pages = page_table.shape
    wbuf = win.shape[1]
    gs = pltpu.PrefetchScalarGridSpec(
        num_scalar_prefetch=2, grid=(n_seq,),
        in_specs=[pl.BlockSpec((1, HPAD, NSA_D), lambda s, pt, ix: (s, 0, 0)),
                  pl.BlockSpec((1, HPAD, 3), lambda s, pt, ix: (s, 0, 0)),
                  pl.BlockSpec((1, HPAD, NSA_D), lambda s, pt, ix: (s, 0, 0)),
                  pl.BlockSpec((1, 1, 512), lambda s, pt, ix: (s, 0, 0)),
                  pl.BlockSpec((1, wbuf, LANE), lambda s, pt, ix: (layer * n_seq + s, 0, 0)),
                  pl.BlockSpec(memory_space=pl.ANY)],
        out_specs=pl.BlockSpec((1, HPAD, NSA_D), lambda s, pt, ix: (s, 0, 0)),
        scratch_shapes=[pltpu.VMEM((2, SEL_TOPK * SEL_BLOCK, LANE), F32), pltpu.SemaphoreType.DMA((2,))])
    return pl.pallas_call(
        functools.partial(_nsa_sel_body, layer=layer, t_pos=n_pages * PAGE),
        grid_spec=gs, out_shape=jax.ShapeDtypeStruct((n_seq, HPAD, NSA_D), F32),
        compiler_params=_cp(("arbitrary",)), name="nsa_decode_sel")(
            page_table, idx, q16, g16, o_c, new_rows, win, cache4)


def _mla_s_body(pt_ref, q_ref, new_ref, cache_ref, o_ref, buf_ref, sem_ref, *, layer, n_chunks):
    q = q_ref[0]
    width = MLA_KVR + MLA_DR
    qk = q[:, :width]

    def compute(c, slot, carry):
        m, l, acc = carry
        kb = buf_ref[slot].astype(BF16)
        s = _dot_nt(qk, kb)
        m_new = jnp.maximum(m, jnp.max(s, axis=-1, keepdims=True))
        alpha = jnp.exp(m - m_new)
        p = jnp.exp(s - m_new)
        l = alpha * l + jnp.sum(p, axis=-1, keepdims=True)
        acc = alpha * acc + _dot(p.astype(BF16), kb[:, :MLA_KVR])
        return m_new, l, acc

    body = _paged_pipeline(cache_ref, layer, pt_ref, buf_ref, sem_ref, n_chunks, None, compute)
    m, l, acc = lax.fori_loop(0, n_chunks, body, (jnp.full((HPAD, 1), NEG, F32), jnp.zeros((HPAD, 1), F32),
                                                  jnp.zeros((HPAD, MLA_KVR), F32)))
    k_new = new_ref[0].astype(F32)
    s_new = jnp.sum(q.astype(F32) * k_new, axis=-1, keepdims=True)
    m_new = jnp.maximum(m, s_new)
    alpha = jnp.exp(m - m_new)
    e_new = jnp.exp(s_new - m_new)
    o_ref[0] = (alpha * acc + e_new * k_new[:, :MLA_KVR]) / (alpha * l + e_new)


def mla_decode(q16, k_new, page_table, cache_mla, layer):
    n_seq, n_pages = page_table.shape
    width = MLA_KVR + MLA_DR
    gs = pltpu.PrefetchScalarGridSpec(
        num_scalar_prefetch=1, grid=(n_seq,),
        in_specs=[pl.BlockSpec((1, HPAD, MLA_QW), lambda s, pt: (s, 0, 0)),
                  pl.BlockSpec((1, 1, MLA_QW), lambda s, pt: (s, 0, 0)),
                  pl.BlockSpec(memory_space=pl.ANY)],
        out_specs=pl.BlockSpec((1, HPAD, MLA_KVR), lambda s, pt: (s, 0, 0)),
        scratch_shapes=[pltpu.VMEM((2, CH_ROWS, width), F32), pltpu.SemaphoreType.DMA((2,))])
    return pl.pallas_call(
        functools.partial(_mla_s_body, layer=layer, n_chunks=n_pages // CH_PAGES),
        grid_spec=gs, out_shape=jax.ShapeDtypeStruct((n_seq, HPAD, MLA_KVR), F32),
        compiler_params=_cp(("arbitrary",)), name="mla_decode")(page_table, q16, k_new, cache_mla)


def _merge_body(gla_ref, nsa_ref, rg_ref, lat_ref, mg_ref, x_ref, wuv_ref, wb_ref, wo_ref, g_ref, b_ref, o_ref):
    o_mla = _dot(lat_ref[...], wuv_ref[...])
    branches = (gla_ref[...], nsa_ref[...], rg_ref[...], o_mla)
    merged = None
    for n, br in enumerate(branches):
        proj = _dot(br.astype(BF16), wb_ref[n])
        term = _sigmoid(mg_ref[:, n * D_MODEL:(n + 1) * D_MODEL]) * proj
        merged = term if merged is None else merged + term
    y = DN_ALPHA * x_ref[...] + _dot(merged.astype(BF16), wo_ref[...])
    o_ref[...] = _layer_norm(y, g_ref[...], b_ref[...])


def merge(o_gla, o_nsa, o_rg, o_lat, z, x, lw):
    n = x.shape[0]
    tm = min(n, 256)
    row = lambda w: pl.BlockSpec((tm, w), lambda i: (i, 0))
    full = lambda a: pl.BlockSpec(a.shape, lambda i: (0,) * a.ndim)
    ws = (lw["w_uv_bd"], lw["w_branch"], lw["w_out"], lw["ln1_g"], lw["ln1_b"])
    return pl.pallas_call(
        _merge_body, grid=(n // tm,),
        in_specs=[row(512), row(512), row(512), row(MLA_H * MLA_KVR), row(N_BRANCH * D_MODEL), row(D_MODEL)]
        + [full(a) for a in ws],
        out_specs=row(D_MODEL), out_shape=jax.ShapeDtypeStruct((n, D_MODEL), F32),
        compiler_params=_cp(("parallel",)), name="merge")(o_gla, o_nsa, o_rg, o_lat, z, x, *ws)


def _ffn_body(x_ref, wu_ref, wd_ref, g_ref, b_ref, o_ref, xb_ref, acc_ref):
    f = pl.program_id(1)

    @pl.when(f == 0)
    def _():
        xb_ref[...] = x_ref[...].astype(BF16)
        acc_ref[...] = jnp.zeros_like(acc_ref)

    h = jnp.maximum(_dot(xb_ref[...], wu_ref[...]), 0.0)
    acc_ref[...] += _dot((h * h).astype(BF16), wd_ref[...])

    @pl.when(f == pl.num_programs(1) - 1)
    def _():
        o_ref[...] = _layer_norm(DN_ALPHA * x_ref[...] + acc_ref[...], g_ref[...], b_ref[...])


def ffn(x, lw):
    n = x.shape[0]
    tm, tf = min(n, 1024), 1024
    vec = pl.BlockSpec((1, D_MODEL), lambda i, f: (0, 0))
    return pl.pallas_call(
        _ffn_body, grid=(n // tm, D_FF // tf),
        in_specs=[pl.BlockSpec((tm, D_MODEL), lambda i, f: (i, 0)),
                  pl.BlockSpec((D_MODEL, tf), lambda i, f: (0, f)),
                  pl.BlockSpec((tf, D_MODEL), lambda i, f: (f, 0)), vec, vec],
        out_specs=pl.BlockSpec((tm, D_MODEL), lambda i, f: (i, 0)),
        out_shape=jax.ShapeDtypeStruct((n, D_MODEL), F32),
        scratch_shapes=[pltpu.VMEM((tm, D_MODEL), BF16), pltpu.VMEM((tm, D_MODEL), F32)],
        compiler_params=_cp(("parallel", "arbitrary")), name="ffn")(x, lw["w_up"], lw["w_down"], lw["ln2_g"],
                                                                      lw["ln2_b"])


def _pad_cols(a, width):
    return jnp.pad(a, ((0, 0), (0, width - a.shape[1])))


def _block_diag(blocks):
    n, r, c = blocks.shape
    eye = jnp.eye(n, dtype=blocks.dtype)
    return (eye[:, None, :, None] * blocks[:, :, None, :]).reshape(n * r, n * c)


def _layer_weights(l, w_in, gla_w_a2, gla_b_a, gla_norm_g, nsa_cmp_logits, rg_conv_w, rg_conv_b, rg_w_r, rg_b_r,
                   rg_w_i, rg_b_i, rg_lambda, mla_q_g, mla_w_uq, mla_kv_g, mla_w_uk, mla_w_uv, w_branch, w_out,
                   ln1_g, ln1_b, w_up, w_down, ln2_g, ln2_b):
    (gq, gk, gv, glr, gr, nq, nkv, ng, rx, rgt, mq, mkv, mkr, mg) = jnp.split(w_in[l], IN_SPLITS, axis=1)
    w_in_p = jnp.concatenate(
        [mg, gv, gr, nq, rx, rgt, _pad_cols(nkv, 512), _pad_cols(mq, 512), gq, gk, mkv,
         _pad_cols(glr, LANE), _pad_cols(ng, LANE), _pad_cols(mkr, LANE)], axis=1)
    w_in_p = _pad_cols(w_in_p, DZ).astype(BF16)
    uq = mla_w_uq[l].reshape(MLA_QR, MLA_H, MLA_DN + MLA_DR)
    uq_nope = uq[:, :, :MLA_DN].reshape(MLA_QR, MLA_H * MLA_DN)
    uq_rope = jnp.pad(uq[:, :, MLA_DN:], ((0, 0), (0, 0), (0, LANE - MLA_DR))).reshape(MLA_QR, MLA_H * LANE)
    pad_rows = lambda a: jnp.pad(a, ((0, 512 - MLA_QR), (0, 0)))
    a2p = jnp.pad(gla_w_a2[l], ((0, LANE - GLA_RANK), (0, 0)))
    return {
        "w_in_p": w_in_p,
        "gla_w_a2p": a2p,
        "gla_b_a": gla_b_a[l][None],
        "gla_norm_g": gla_norm_g[l].reshape(1, GLA_H * GLA_DV),
        "gla_w_a2h": a2p.reshape(LANE, GLA_H, GLA_DK).transpose(1, 0, 2),
        "gla_b_ah": gla_b_a[l].reshape(GLA_H, 1, GLA_DK),
        "gla_norm_gh": gla_norm_g[l].reshape(GLA_H, 1, GLA_DV),
        "nsa_cmp_logits": nsa_cmp_logits[l],
        "rg_conv_w": rg_conv_w[l], "rg_conv_b": rg_conv_b[l][None],
        "rg_w_r_bd": _block_diag(rg_w_r[l]).astype(BF16), "rg_b_r": rg_b_r[l][None],
        "rg_w_i_bd": _block_diag(rg_w_i[l]).astype(BF16), "rg_b_i": rg_b_i[l][None],
        "rg_lambda": rg_lambda[l][None],
        "mla_q_g": _pad_cols(mla_q_g[l][None], 512),
        "w_uq_nope": pad_rows(uq_nope).astype(BF16),
        "w_uq_rope": pad_rows(uq_rope).astype(BF16),
        "w_uk_bd": _block_diag(mla_w_uk[l].transpose(1, 2, 0)).astype(BF16),
        "mla_kv_g": mla_kv_g[l][None],
        "w_uv_bd": _block_diag(mla_w_uv[l].transpose(1, 0, 2)).astype(BF16),
        "w_branch": w_branch[l].astype(BF16), "w_out": w_out[l].astype(BF16),
        "ln1_g": ln1_g[l][None], "ln1_b": ln1_b[l][None],
        "w_up": w_up[l].astype(BF16), "w_down": w_down[l].astype(BF16),
        "ln2_g": ln2_g[l][None], "ln2_b": ln2_b[l][None],
    }


def _rope_tables(pos):
    half = MLA_DR // 2
    freq = ROPE_BASE ** (-jnp.arange(half, dtype=F32) / half)
    ang = pos.astype(F32)[:, None] * freq
    cos, sin = jnp.cos(ang), jnp.sin(ang)
    zero = jnp.zeros_like(cos)
    pad = lambda a, b: jnp.pad(jnp.concatenate([a, b], axis=1), ((0, 0), (0, LANE - MLA_DR)))
    return pad(cos, cos), pad(-sin, zero), pad(zero, sin)


def _prompt_layer(x, lw, tabs, batch, seq_len):
    z = in_proj(x, lw["w_in_p"])
    qm, mla_rows, km = attn_prep(z, tabs, lw, seq_len)
    o_gla, s_gla = gla_prompt(z, lw, batch, seq_len)
    o_rg, rg_h, rg_buf = rglru_prompt(z, lw, batch, seq_len)
    o_nsa = nsa_prompt(z, lw, batch, seq_len)
    o_lat = mla_prompt(qm, km, batch, seq_len)
    x = ffn(merge(o_gla, o_nsa, o_rg, o_lat, z, x, lw), lw)
    nkv = z[:, C_NKV:C_NKV + 6 * NSA_D].reshape(batch, seq_len, 6, NSA_D)
    keep = min(WINDOW, seq_len)
    states = (nkv[:, :, :4], nkv[:, seq_len - keep:, 4:], mla_rows.reshape(batch, seq_len, -1), s_gla,
              rg_h.reshape(batch, RG_W), rg_buf)
    return x, states


def _sample_layer(x, lw, tabs, layer, cache4, win3, win_l, cache_mla, page_table, s_gla, rg_buf, rg_h):
    n_seq = x.shape[0]
    t_pos = page_table.shape[1] * PAGE
    z = in_proj(x, lw["w_in_p"])
    qm, mla_rows, km = attn_prep(z, tabs, lw, 1)
    o_gla, s_gla_new = gla_decode(z, s_gla, lw)
    o_rg, rg_buf_new, rg_h_new = rglru_decode(z, rg_buf, rg_h, lw)
    pad_h = lambda a: jnp.pad(a, ((0, 0), (0, HPAD - a.shape[1]), (0, 0)))
    q16 = pad_h(z[:, C_NQ:C_NQ + NSA_H * NSA_D].reshape(n_seq, NSA_H, NSA_D))
    g16 = pad_h(z[:, C_NG:C_NG + 3 * NSA_H].reshape(n_seq, NSA_H, 3))
    o_c, imp = nsa_decode_cmp(q16, page_table, cache4, lw, layer)
    idx = nsa_decode_topk(imp.reshape(n_seq, -1), t_pos)[:, :SEL_TOPK]
    new_rows = z[:, None, C_NKV:C_NKV + 512]
    o_nsa = nsa_decode_sel(q16, g16, o_c, new_rows, win3, page_table, idx, cache4, layer)
    o_nsa = o_nsa[:, :NSA_H].reshape(n_seq, NSA_H * NSA_D)
    qm16 = pad_h(qm.transpose(1, 0, 2))
    o_lat = mla_decode(qm16, km[:, None], page_table, cache_mla, layer)
    o_lat = o_lat[:, :MLA_H].reshape(n_seq, MLA_H * MLA_KVR).astype(BF16)
    x = ffn(merge(o_gla, o_nsa, o_rg, o_lat, z, x, lw), lw)
    nkv = z[:, C_NKV:C_NKV + 6 * NSA_D].reshape(n_seq, 1, 6, NSA_D)
    wkv = jnp.concatenate([win_l, nkv[:, :, 4:]], axis=1)
    keep = min(WINDOW, wkv.shape[1])
    states = (nkv[:, :, :4], wkv[:, wkv.shape[1] - keep:], mla_rows.reshape(n_seq, 1, -1), s_gla_new, rg_h_new,
              rg_buf_new)
    return x, states


def kernel(x_prompt, x_sample, cache_nsa, cache_nsa_win, cache_mla, state_gla, state_rg_h, state_rg_conv,
           page_table, w_in, gla_w_a2, gla_b_a, gla_norm_g, nsa_cmp_logits, rg_conv_w, rg_conv_b, rg_w_r,
           rg_b_r, rg_w_i, rg_b_i, rg_lambda, mla_q_g, mla_w_uq, mla_kv_g, mla_w_uk, mla_w_uv, w_branch,
           w_out, ln1_g, ln1_b, w_up, w_down, ln2_g, ln2_b):
    batch, seq_len, _ = x_prompt.shape
    n_seq, dec_seq, _ = x_sample.shape
    assert dec_seq == 1 and seq_len % 512 == 0 and page_table.shape[1] % CH_PAGES == 0
    past = page_table.shape[1] * PAGE
    depth = w_in.shape[0]
    params = (w_in, gla_w_a2, gla_b_a, gla_norm_g, nsa_cmp_logits, rg_conv_w, rg_conv_b, rg_w_r, rg_b_r, rg_w_i,
              rg_b_i, rg_lambda, mla_q_g, mla_w_uq, mla_kv_g, mla_w_uk, mla_w_uv, w_branch, w_out, ln1_g, ln1_b,
              w_up, w_down, ln2_g, ln2_b)
    tabs_p = _rope_tables(jnp.arange(seq_len, dtype=jnp.int32))
    tabs_s = _rope_tables(jnp.full((n_seq,), past, jnp.int32))
    n_phys = cache_nsa.shape[1]
    cache4 = cache_nsa.reshape(depth, n_phys, PAGE, 4 * NSA_D)
    wbuf = cache_nsa_win.shape[2]
    win3 = cache_nsa_win.reshape(depth * n_seq, wbuf, 2 * NSA_D)
    yp = x_prompt.reshape(batch * seq_len, D_MODEL)
    ys = x_sample.reshape(n_seq, D_MODEL)
    st_p, st_s = [], []
    for l in range(depth):
        lw = _layer_weights(l, *params)
        yp, sp = _prompt_layer(yp, lw, tabs_p, batch, seq_len)
        ys, ss = _sample_layer(ys, lw, tabs_s, l, cache4, win3, cache_nsa_win[l], cache_mla, page_table,
                               state_gla[l], state_rg_conv[l], state_rg_h[l])
        st_p.append(sp)
        st_s.append(ss)
    stack = lambda sts, i: jnp.stack([s[i] for s in sts])
    return (yp.reshape(batch, seq_len, D_MODEL), ys.reshape(n_seq, 1, D_MODEL),
            stack(st_p, 0), stack(st_s, 0), stack(st_p, 1), stack(st_s, 1), stack(st_p, 2), stack(st_s, 2),
            stack(st_p, 3), stack(st_s, 3), stack(st_p, 4), stack(st_s, 4), stack(st_p, 5), stack(st_s, 5))
```

```python
import functools

import numpy as np
import jax
import jax.numpy as jnp
from jax import lax
from jax.experimental import pallas as pl
from jax.experimental.pallas import tpu as pltpu

F32 = jnp.float32
BF16 = jnp.bfloat16
HIGHEST = lax.Precision.HIGHEST

D_MODEL = 1024
DEPTH = 2
PAGE = 128
GLA_H, GLA_DK, GLA_DV, GLA_RANK, GLA_TAU, GLA_CHUNK = 4, 64, 128, 16, 16.0, 64
NSA_H, NSA_D = 8, 64
NSA_SCALE = NSA_D ** -0.5
CMP_BLOCK, SEL_BLOCK, SEL_TOPK, WINDOW, QBLOCK = 32, 64, 16, 512, 128
RG_W, RG_BLOCKS, CONV_W, RG_C = 512, 8, 4, 8.0
RG_BD = RG_W // RG_BLOCKS
MLA_H, MLA_QR, MLA_KVR, MLA_DN, MLA_DR, MLA_DV = 8, 384, 256, 64, 32, 64
MLA_SCALE = (MLA_DN + MLA_DR) ** -0.5
ROPE_BASE = 10000.0
N_BRANCH, BRANCH_W = 4, 512
D_FF = 4 * D_MODEL
DN_ALPHA = (2 * DEPTH) ** 0.25
LN_EPS = 1e-5
IN_SIZES = (GLA_H * GLA_DK, GLA_H * GLA_DK, GLA_H * GLA_DV, GLA_RANK, GLA_H * GLA_DV,
            NSA_H * NSA_D, 6 * NSA_D, 3 * NSA_H, RG_W, RG_W, MLA_QR, MLA_KVR, MLA_DR, N_BRANCH * D_MODEL)
IN_SPLITS = tuple(int(v) for v in np.cumsum(IN_SIZES)[:-1])

LANE = 128
VMEM_LIMIT = 56 * 2 ** 20
NEG = -1e30
MLA_QW = MLA_KVR + LANE

C_MG, C_GV, C_GR, C_NQ, C_RX, C_RGT, C_NKV, C_MQ = 0, 4096, 4608, 5120, 5632, 6144, 6656, 7168
C_GQ, C_GK, C_MKV, C_GLR, C_NG, C_MKR = 7680, 7936, 8192, 8448, 8576, 8704
DZ = 9216


def _cp(sem, vmem=VMEM_LIMIT):
    return pltpu.CompilerParams(dimension_semantics=sem, vmem_limit_bytes=vmem)


def _dot(a, b, precision=None):
    return jnp.dot(a, b, preferred_element_type=F32, precision=precision)


def _dot_nt(a, b):
    return lax.dot_general(a, b, (((1,), (1,)), ((), ())), preferred_element_type=F32)


def _dot_tn(a, b, precision=None):
    return lax.dot_general(a, b, (((0,), (0,)), ((), ())), preferred_element_type=F32, precision=precision)


def _log_sigmoid(x):
    return jnp.minimum(x, 0.0) - jnp.log(1.0 + jnp.exp(-jnp.abs(x)))


def _sigmoid(x):
    return 1.0 / (1.0 + jnp.exp(-x))


def _layer_norm(x, g, b):
    mu = jnp.mean(x, axis=-1, keepdims=True)
    xc = x - mu
    var = jnp.mean(xc * xc, axis=-1, keepdims=True)
    return xc * lax.rsqrt(var + LN_EPS) * g + b


def _alibi_slope(h):
    return lax.bitcast_convert_type((126 - h) << 23, F32)


def _softmax_rows(s, valid):
    s = jnp.where(valid, s, -jnp.inf)
    m = jnp.max(s, axis=-1, keepdims=True)
    m = jnp.where(m > -jnp.inf, m, 0.0)
    e = jnp.exp(s - m)
    d = jnp.sum(e, axis=-1, keepdims=True)
    return e / jnp.where(d > 0, d, 1.0)


def _inproj_body(x_ref, w_ref, o_ref, xb_ref):
    @pl.when(pl.program_id(1) == 0)
    def _():
        xb_ref[...] = x_ref[...].astype(BF16)

    o_ref[...] = _dot(xb_ref[...], w_ref[...])


def in_proj(x, w):
    n = x.shape[0]
    tm, tn = min(n, 1024), 1024
    return pl.pallas_call(
        _inproj_body, grid=(n // tm, DZ // tn),
        in_specs=[pl.BlockSpec((tm, D_MODEL), lambda i, j: (i, 0)),
                  pl.BlockSpec((D_MODEL, tn), lambda i, j: (0, j))],
        out_specs=pl.BlockSpec((tm, tn), lambda i, j: (i, j)),
        out_shape=jax.ShapeDtypeStruct((n, DZ), F32),
        scratch_shapes=[pltpu.VMEM((tm, D_MODEL), BF16)],
        compiler_params=_cp(("parallel", "arbitrary")), name="in_proj")(x, w)


def _rope_lanes(x, cos, sa, sb):
    w = x.shape[-1]
    return x * cos + pltpu.roll(x, w - 16, 1) * sa + pltpu.roll(x, 16, 1) * sb


def _prep_body(mq_ref, mkv_ref, mkr_ref, cos_ref, sa_ref, sb_ref, qg_ref, wn_ref, wr_ref, wuk_ref, kvg_ref,
               qm_ref, rows_ref, km_ref):
    mq = mq_ref[...]
    qn = mq * lax.rsqrt(jnp.sum(mq * mq, axis=-1, keepdims=True) * (1.0 / MLA_QR) + 1e-6) * qg_ref[...]
    qb = qn.astype(BF16)
    nope = _dot(qb, wn_ref[...])
    rq = _dot(qb, wr_ref[...])
    qlat = _dot(nope.astype(BF16), wuk_ref[...])
    cos, sa, sb = cos_ref[...], sa_ref[...], sb_ref[...]
    rq = _rope_lanes(rq, jnp.concatenate([cos] * MLA_H, axis=1), jnp.concatenate([sa] * MLA_H, axis=1),
                     jnp.concatenate([sb] * MLA_H, axis=1))
    for h in range(MLA_H):
        qh = jnp.concatenate([qlat[:, h * MLA_KVR:(h + 1) * MLA_KVR], rq[:, h * LANE:(h + 1) * LANE]], axis=1)
        qm_ref[h] = (qh * MLA_SCALE).astype(BF16)
    mkv = mkv_ref[...]
    ckv = mkv * lax.rsqrt(jnp.mean(mkv * mkv, axis=-1, keepdims=True) + 1e-6) * kvg_ref[...]
    kr = _rope_lanes(mkr_ref[...], cos, sa, sb)
    rows_ref[...] = jnp.concatenate([ckv, kr[:, :MLA_DR]], axis=1)
    km_ref[...] = jnp.concatenate([ckv, kr], axis=1).astype(BF16)


def attn_prep(z, tabs, lw, seq_len):
    n = z.shape[0]
    tm = min(n, 512)
    nt = seq_len // tm if seq_len >= tm else 1
    tab_spec = pl.BlockSpec((tm, LANE), lambda i: (i % nt, 0))
    full = lambda a: pl.BlockSpec(a.shape, lambda i: (0,) * a.ndim)
    ws = (lw["mla_q_g"], lw["w_uq_nope"], lw["w_uq_rope"], lw["w_uk_bd"], lw["mla_kv_g"])
    return pl.pallas_call(
        _prep_body, grid=(n // tm,),
        in_specs=[pl.BlockSpec((tm, 512), lambda i: (i, C_MQ // 512)),
                  pl.BlockSpec((tm, 256), lambda i: (i, C_MKV // 256)),
                  pl.BlockSpec((tm, LANE), lambda i: (i, C_MKR // LANE)),
                  tab_spec, tab_spec, tab_spec] + [full(a) for a in ws],
        out_specs=[pl.BlockSpec((MLA_H, tm, MLA_QW), lambda i: (0, i, 0)),
                   pl.BlockSpec((tm, MLA_KVR + MLA_DR), lambda i: (i, 0)),
                   pl.BlockSpec((tm, MLA_QW), lambda i: (i, 0))],
        out_shape=[jax.ShapeDtypeStruct((MLA_H, n, MLA_QW), BF16),
                   jax.ShapeDtypeStruct((n, MLA_KVR + MLA_DR), F32),
                   jax.ShapeDtypeStruct((n, MLA_QW), BF16)],
        compiler_params=_cp(("parallel",)), name="attn_prep")(z, z, z, *tabs, *ws)


def _gla_out(o, gr, ng):
    outs = []
    for h in range(GLA_H):
        oh = o[:, h * GLA_DV:(h + 1) * GLA_DV]
        outs.append(oh * lax.rsqrt(jnp.mean(oh * oh, axis=-1, keepdims=True) + 1e-6))
    y = jnp.concatenate(outs, axis=1) * ng
    return y * (gr * _sigmoid(gr))


def _gla_p_body(q_ref, k_ref, v_ref, gr_ref, glr_ref, wa_ref, ba_ref, ng_ref, o_ref, st_ref, s_ref, *, tile):
    t = pl.program_id(1)
    c = GLA_CHUNK

    @pl.when(t == 0)
    def _():
        s_ref[...] = jnp.zeros_like(s_ref)

    la = _log_sigmoid(_dot(glr_ref[...], wa_ref[...], HIGHEST) + ba_ref[...]) * (1.0 / GLA_TAU)
    row = lax.broadcasted_iota(jnp.int32, (c, c), 0)
    col = lax.broadcasted_iota(jnp.int32, (c, c), 1)
    tri = (row >= col).astype(F32)
    ones = jnp.ones((c, GLA_DV), F32)
    for ci in range(tile // c):
        rows = slice(ci * c, (ci + 1) * c)
        g = la[rows]
        b = _dot(tri, g, HIGHEST)
        blast_rep = _dot_tn(g, ones, HIGHEST)
        q = q_ref[rows, :] * (GLA_DK ** -0.5)
        k = k_ref[rows, :]
        v = v_ref[rows, :]
        outs = []
        for h in range(GLA_H):
            dk = slice(h * GLA_DK, (h + 1) * GLA_DK)
            bh, qh, kh = b[:, dk], q[:, dk], k[:, dk]
            vh = v[:, h * GLA_DV:(h + 1) * GLA_DV].astype(BF16)
            bmid = bh[c // 2 - 1:c // 2]
            blast = bh[c - 1:c]
            s_old = s_ref[h]
            o_inter = _dot((qh * jnp.exp(bh)).astype(BF16), s_old.astype(BF16))
            att = _dot_nt((qh * jnp.exp(bh - bmid)).astype(BF16), (kh * jnp.exp(bmid - bh)).astype(BF16)) * tri
            o_intra = _dot(att.astype(BF16), vh)
            kl = (kh * jnp.exp(blast - bh)).astype(BF16)
            s_ref[h] = jnp.exp(blast_rep[dk]) * s_old + _dot_tn(kl, vh)
            outs.append(o_inter + o_intra)
        o_ref[rows, :] = _gla_out(jnp.concatenate(outs, axis=1), gr_ref[rows, :], ng_ref[...])

    @pl.when(t == pl.num_programs(1) - 1)
    def _():
        st_ref[0] = s_ref[...]


def gla_prompt(z, lw, batch, seq_len):
    tile = 256
    nt = seq_len // tile
    rb = lambda b, t: b * nt + t
    full = lambda a: pl.BlockSpec(a.shape, lambda b, t: (0,) * a.ndim)
    ws = (lw["gla_w_a2p"], lw["gla_b_a"], lw["gla_norm_g"])
    return pl.pallas_call(
        functools.partial(_gla_p_body, tile=tile), grid=(batch, nt),
        in_specs=[pl.BlockSpec((tile, 256), lambda b, t: (rb(b, t), C_GQ // 256)),
                  pl.BlockSpec((tile, 256), lambda b, t: (rb(b, t), C_GK // 256)),
                  pl.BlockSpec((tile, 512), lambda b, t: (rb(b, t), C_GV // 512)),
                  pl.BlockSpec((tile, 512), lambda b, t: (rb(b, t), C_GR // 512)),
                  pl.BlockSpec((tile, LANE), lambda b, t: (rb(b, t), C_GLR // LANE))] + [full(a) for a in ws],
        out_specs=[pl.BlockSpec((tile, 512), lambda b, t: (rb(b, t), 0)),
                   pl.BlockSpec((1, GLA_H, GLA_DK, GLA_DV), lambda b, t: (b, 0, 0, 0))],
        out_shape=[jax.ShapeDtypeStruct((batch * seq_len, 512), F32),
                   jax.ShapeDtypeStruct((batch, GLA_H, GLA_DK, GLA_DV), F32)],
        scratch_shapes=[pltpu.VMEM((GLA_H, GLA_DK, GLA_DV), F32)],
        compiler_params=_cp(("arbitrary", "arbitrary")), name="gla_prompt")(z, z, z, z, z, *ws)


def _gla_s_body(q_ref, k_ref, v_ref, gr_ref, glr_ref, wa_ref, ba_ref, ng_ref, st_ref, o_ref, sto_ref):
    la = _log_sigmoid(_dot(glr_ref[...], wa_ref[0], HIGHEST) + ba_ref[0]) * (1.0 / GLA_TAU)
    ea = jnp.exp(la)
    q = q_ref[0] * (GLA_DK ** -0.5)
    k = k_ref[0]
    v = v_ref[...]
    o = jnp.zeros(v.shape, F32)
    for d in range(GLA_DK):
        s_new = ea[:, d:d + 1] * st_ref[:, d, :] + k[:, d:d + 1] * v
        sto_ref[:, d, :] = s_new
        o = o + q[:, d:d + 1] * s_new
    y = o * lax.rsqrt(jnp.mean(o * o, axis=-1, keepdims=True) + 1e-6) * ng_ref[0]
    gr = gr_ref[...]
    o_ref[...] = y * (gr * _sigmoid(gr))


def gla_decode(z, state, lw):
    s = z.shape[0]
    heads = lambda a: a.reshape(s, GLA_H, GLA_DK).transpose(1, 0, 2)
    qh = heads(z[:, C_GQ:C_GQ + 256])
    kh = heads(z[:, C_GK:C_GK + 256])
    st = state.reshape(s, GLA_H * GLA_DK, GLA_DV)
    o, st_new = pl.pallas_call(
        _gla_s_body, grid=(GLA_H,),
        in_specs=[pl.BlockSpec((1, s, GLA_DK), lambda h: (h, 0, 0)),
                  pl.BlockSpec((1, s, GLA_DK), lambda h: (h, 0, 0)),
                  pl.BlockSpec((s, GLA_DV), lambda h: (0, C_GV // GLA_DV + h)),
                  pl.BlockSpec((s, GLA_DV), lambda h: (0, C_GR // GLA_DV + h)),
                  pl.BlockSpec((s, LANE), lambda h: (0, C_GLR // LANE)),
                  pl.BlockSpec((1, LANE, GLA_DK), lambda h: (h, 0, 0)),
                  pl.BlockSpec((1, 1, GLA_DK), lambda h: (h, 0, 0)),
                  pl.BlockSpec((1, 1, GLA_DV), lambda h: (h, 0, 0)),
                  pl.BlockSpec((s, GLA_DK, GLA_DV), lambda h: (0, h, 0))],
        out_specs=[pl.BlockSpec((s, GLA_DV), lambda h: (0, h)),
                   pl.BlockSpec((s, GLA_DK, GLA_DV), lambda h: (0, h, 0))],
        out_shape=[jax.ShapeDtypeStruct((s, GLA_H * GLA_DV), F32),
                   jax.ShapeDtypeStruct(st.shape, F32)],
        compiler_params=_cp(("arbitrary",)), name="gla_decode")(
            qh, kh, z, z, z, lw["gla_w_a2h"], lw["gla_b_ah"], lw["gla_norm_gh"], st)
    return o, st_new.reshape(state.shape)


def _rg_gates(xc, wr_ref, br_ref, wi_ref, bi_ref, lam_ref):
    xb = xc.astype(BF16)
    r = _sigmoid(_dot(xb, wr_ref[...]) + br_ref[...])
    i = _sigmoid(_dot(xb, wi_ref[...]) + bi_ref[...])
    log_a = RG_C * r * _log_sigmoid(lam_ref[...])
    a = jnp.exp(log_a)
    u = jnp.sqrt(1.0 - jnp.exp(2.0 * log_a)) * (i * xc)
    return a, u


def _gelu_tanh(x):
    return 0.5 * x * (1.0 + jnp.tanh(0.7978845608028654 * (x + 0.044715 * x * x * x)))


def _rg_p_body(x_ref, g_ref, cw_ref, cb_ref, wr_ref, br_ref, wi_ref, bi_ref, lam_ref,
               y_ref, h_ref, buf_ref, xe_ref, hc_ref, *, tile):
    t = pl.program_id(1)

    @pl.when(t == 0)
    def _():
        xe_ref[0:8, :] = jnp.zeros((8, RG_W), F32)
        hc_ref[...] = jnp.zeros_like(hc_ref)

    x = x_ref[...]
    xe_ref[8:8 + tile, :] = x
    cw = cw_ref[...]
    xc = cb_ref[...] + x * cw[3:4]
    for j in range(1, CONV_W):
        xc = xc + xe_ref[8 - j:8 - j + tile, :] * cw[3 - j:4 - j]
    a, u = _rg_gates(xc, wr_ref, br_ref, wi_ref, bi_ref, lam_ref)
    row = lax.broadcasted_iota(jnp.int32, (tile, RG_W), 0)
    s = 1
    while s < tile:
        keep = row >= s
        a_prev = jnp.where(keep, pltpu.roll(a, s, 0), 1.0)
        u_prev = jnp.where(keep, pltpu.roll(u, s, 0), 0.0)
        u = a * u_prev + u
        a = a * a_prev
        s *= 2
    h = u + a * hc_ref[...]
    hc_ref[...] = h[tile - 1:tile]
    xe_ref[0:8, :] = x[tile - 8:tile]
    y_ref[...] = h * _gelu_tanh(g_ref[...])

    @pl.when(t == pl.num_programs(1) - 1)
    def _():
        h_ref[0] = h[tile - 1:tile]
        buf_ref[0] = x[tile - (CONV_W - 1):tile]


def rglru_prompt(z, lw, batch, seq_len):
    tile = 256
    nt = seq_len // tile
    rb = lambda b, t: b * nt + t
    full = lambda a: pl.BlockSpec(a.shape, lambda b, t: (0,) * a.ndim)
    ws = (lw["rg_conv_w"], lw["rg_conv_b"], lw["rg_w_r_bd"], lw["rg_b_r"], lw["rg_w_i_bd"], lw["rg_b_i"],
          lw["rg_lambda"])
    return pl.pallas_call(
        functools.partial(_rg_p_body, tile=tile), grid=(batch, nt),
        in_specs=[pl.BlockSpec((tile, RG_W), lambda b, t: (rb(b, t), C_RX // RG_W)),
                  pl.BlockSpec((tile, RG_W), lambda b, t: (rb(b, t), C_RGT // RG_W))] + [full(a) for a in ws],
        out_specs=[pl.BlockSpec((tile, RG_W), lambda b, t: (rb(b, t), 0)),
                   pl.BlockSpec((1, 1, RG_W), lambda b, t: (b, 0, 0)),
                   pl.BlockSpec((1, CONV_W - 1, RG_W), lambda b, t: (b, 0, 0))],
        out_shape=[jax.ShapeDtypeStruct((batch * seq_len, RG_W), F32),
                   jax.ShapeDtypeStruct((batch, 1, RG_W), F32),
                   jax.ShapeDtypeStruct((batch, CONV_W - 1, RG_W), F32)],
        scratch_shapes=[pltpu.VMEM((8 + tile, RG_W), F32), pltpu.VMEM((1, RG_W), F32)],
        compiler_params=_cp(("arbitrary", "arbitrary")), name="rglru_prompt")(z, z, *ws)


def _rg_s_body(x_ref, g_ref, b0_ref, b1_ref, b2_ref, h0_ref, cw_ref, cb_ref, wr_ref, br_ref, wi_ref, bi_ref,
               lam_ref, y_ref, h_ref):
    cw = cw_ref[...]
    x = x_ref[...]
    xc = cb_ref[...] + b0_ref[...] * cw[0:1] + b1_ref[...] * cw[1:2] + b2_ref[...] * cw[2:3] + x * cw[3:4]
    a, u = _rg_gates(xc, wr_ref, br_ref, wi_ref, bi_ref, lam_ref)
    h = u + a * h0_ref[...]
    h_ref[...] = h
    y_ref[...] = h * _gelu_tanh(g_ref[...])


def rglru_decode(z, conv_buf, h0, lw):
    s = z.shape[0]
    buf = conv_buf.reshape(s, (CONV_W - 1) * RG_W)
    full = lambda a: pl.BlockSpec(a.shape, lambda i: (0,) * a.ndim)
    ws = (lw["rg_conv_w"], lw["rg_conv_b"], lw["rg_w_r_bd"], lw["rg_b_r"], lw["rg_w_i_bd"], lw["rg_b_i"],
          lw["rg_lambda"])
    blk = lambda c: pl.BlockSpec((s, RG_W), lambda i: (0, c))
    y, h = pl.pallas_call(
        _rg_s_body, grid=(1,),
        in_specs=[blk(C_RX // RG_W), blk(C_RGT // RG_W), blk(0), blk(1), blk(2), blk(0)] + [full(a) for a in ws],
        out_specs=[blk(0), blk(0)],
        out_shape=[jax.ShapeDtypeStruct((s, RG_W), F32), jax.ShapeDtypeStruct((s, RG_W), F32)],
        compiler_params=_cp(("arbitrary",)), name="rglru_decode")(z, z, buf, buf, buf, h0, *ws)
    buf_new = jnp.concatenate([conv_buf[:, 1:], z[:, None, C_RX:C_RX + RG_W]], axis=1)
    return y, buf_new, h


def _cmp_weights(lg):
    e = jnp.exp(lg - jnp.max(lg, axis=-1, keepdims=True))
    return e / (jnp.sum(e, axis=-1, keepdims=True) * (CMP_BLOCK / lg.shape[-1]))


def _topk_mask(vals, lane, k):
    sel = jnp.zeros(vals.shape, jnp.bool_)
    idxs, oks = [], []
    lane = lane.astype(F32)
    big = float(vals.shape[-1])
    for _ in range(k):
        m = jnp.max(vals, axis=-1, keepdims=True)
        idx = jnp.min(jnp.where(vals == m, lane, big), axis=-1, keepdims=True)
        pick = lane == idx
        ok = m > -jnp.inf
        sel = sel | (pick & ok)
        vals = jnp.where(pick, -jnp.inf, vals)
        idxs.append(idx)
        oks.append(ok)
    return sel, idxs, oks


def _topk_rank_mask(vals, n, k):
    vt = vals.T[:n]
    cidx = lax.broadcasted_iota(jnp.int32, (n, 1), 0)
    rank = jnp.zeros(vt.shape, F32)
    for j in range(n):
        rj = vt[j:j + 1]
        ahead = (rj > vt) | ((rj == vt) & (cidx > j))
        rank = rank + jnp.where(ahead, 1.0, 0.0)
    sel_t = (rank < k) & (vt > -jnp.inf)
    return jnp.where(sel_t, 1.0, 0.0).T > 0.5


def _nsa_p_body(q_ref, g_ref, kvc_ref, kvs_ref, kvw_ref, lg_ref, o_ref,
                kcb_ref, vcb_ref, ks_ref, vs_ref, kw_ref, vw_ref, *, seq_len):
    i = pl.program_id(1)
    qb = QBLOCK
    n_cmp = seq_len // CMP_BLOCK
    tk = 512
    rows = NSA_H * qb

    @pl.when(i == 0)
    def _():
        wc = _cmp_weights(lg_ref[...])
        blk = lax.broadcasted_iota(jnp.int32, (n_cmp, seq_len), 0)
        pos = lax.broadcasted_iota(jnp.int32, (n_cmp, seq_len), 1)
        wmat = jnp.where((pos // CMP_BLOCK) == blk, wc, 0.0).astype(BF16)
        kvcb = _dot(wmat, kvc_ref[...].astype(BF16))
        kcb_ref[...] = kvcb[:, :NSA_D].astype(BF16)
        vcb_ref[...] = kvcb[:, NSA_D:].astype(BF16)
        kvs = kvs_ref[...]
        ks_ref[...] = kvs[:, :NSA_D].astype(BF16)
        vs_ref[...] = kvs[:, NSA_D:].astype(BF16)
        kvw = kvw_ref[...]
        kw_ref[...] = kvw[:, :NSA_D].astype(BF16)
        vw_ref[...] = kvw[:, NSA_D:].astype(BF16)

    q = q_ref[...] * NSA_SCALE
    q2 = jnp.concatenate([q[:, h * NSA_D:(h + 1) * NSA_D] for h in range(NSA_H)], axis=0).astype(BF16)
    t0 = i * qb
    tq = t0 + lax.broadcasted_iota(jnp.int32, (qb, 1), 0)
    slope3 = _alibi_slope(lax.broadcasted_iota(jnp.int32, (NSA_H, 1, 1), 0))

    lane_c = lax.broadcasted_iota(jnp.int32, (1, n_cmp), 1)
    dist_c = (tq - (lane_c * CMP_BLOCK + (CMP_BLOCK - 1))).astype(F32)
    s_c = _dot_nt(q2, kcb_ref[...]).reshape(NSA_H, qb, n_cmp) - slope3 * dist_c[None]
    p_c = _softmax_rows(s_c, (dist_c >= 0)[None])
    o_c = _dot(p_c.reshape(rows, n_cmp).astype(BF16), vcb_ref[...])

    imp = jnp.sum(p_c, axis=0)
    n_sel = seq_len // SEL_BLOCK
    pr = lax.broadcasted_iota(jnp.int32, (n_cmp, n_cmp), 0)
    pc = lax.broadcasted_iota(jnp.int32, (n_cmp, n_cmp), 1)
    pair = ((pr // (SEL_BLOCK // CMP_BLOCK)) == pc).astype(F32)
    imp = _dot(imp, pair, HIGHEST)
    forced = (lane_c == tq // SEL_BLOCK) | (lane_c == 0)
    causal = (lane_c * SEL_BLOCK <= tq) & (lane_c < n_sel)
    imp = jnp.where(forced, jnp.inf, jnp.where(causal, imp, -jnp.inf))
    sel_b = _topk_rank_mask(imp, n_sel, min(SEL_TOPK, n_sel)).astype(BF16)

    def sel_tile(j, carry):
        m, l, acc = carry
        k0 = pl.multiple_of(j * tk, tk)
        kt = ks_ref[pl.ds(k0, tk), :]
        vt = vs_ref[pl.ds(k0, tk), :]
        kpos = k0 + lax.broadcasted_iota(jnp.int32, (1, tk), 1)
        er = lax.broadcasted_iota(jnp.int32, (n_sel, tk), 0)
        ec = k0 + lax.broadcasted_iota(jnp.int32, (n_sel, tk), 1)
        expand = ((ec // SEL_BLOCK) == er).astype(BF16)
        ok = (_dot(sel_b, expand) > 0.5) & (kpos <= tq)
        dist = (tq - kpos).astype(F32)
        bias = jnp.where(ok, 0.0, NEG)
        s = _dot_nt(q2, kt).reshape(NSA_H, qb, tk) - slope3 * dist[None] + bias[None]
        s = s.reshape(rows, tk)
        m_new = jnp.maximum(m, jnp.max(s, axis=-1, keepdims=True))
        alpha = jnp.exp(m - m_new)
        p = jnp.exp(s - m_new)
        l = alpha * l + jnp.sum(p, axis=-1, keepdims=True)
        acc = alpha * acc + _dot(p.astype(BF16), vt)
        return m_new, l, acc

    n_tiles = (t0 + qb + tk - 1) // tk
    m0 = jnp.full((rows, 1), NEG, F32)
    _, l_s, acc_s = lax.fori_loop(0, n_tiles, sel_tile, (m0, jnp.zeros((rows, 1), F32), jnp.zeros((rows, NSA_D), F32)))
    o_s = acc_s / l_s

    wk = min(WINDOW + qb, seq_len)
    w0 = pl.multiple_of(jnp.maximum(t0 - WINDOW, 0), qb)
    wpos = w0 + lax.broadcasted_iota(jnp.int32, (1, wk), 1)
    dist_w = tq - wpos
    ok_w = (dist_w >= 0) & (dist_w <= WINDOW)
    s_w = _dot_nt(q2, kw_ref[pl.ds(w0, wk), :]).reshape(NSA_H, qb, wk) - slope3 * dist_w.astype(F32)[None]
    p_w = _softmax_rows(s_w, ok_w[None])
    o_w = _dot(p_w.reshape(rows, wk).astype(BF16), vw_ref[pl.ds(w0, wk), :])

    gi = _sigmoid(g_ref[...])
    outs = []
    for h in range(NSA_H):
        r = slice(h * qb, (h + 1) * qb)
        outs.append(gi[:, 3 * h:3 * h + 1] * o_c[r] + gi[:, 3 * h + 1:3 * h + 2] * o_s[r]
                    + gi[:, 3 * h + 2:3 * h + 3] * o_w[r])
    o_ref[...] = jnp.concatenate(outs, axis=1)


def nsa_prompt(z, lw, batch, seq_len):
    nq = seq_len // QBLOCK
    n_cmp = seq_len // CMP_BLOCK
    rb = lambda b, i: b * nq + i
    kv = lambda c: pl.BlockSpec((seq_len, LANE), lambda b, i: (b, c))
    lg = jnp.tile(lw["nsa_cmp_logits"], seq_len // CMP_BLOCK)[None]
    return pl.pallas_call(
        functools.partial(_nsa_p_body, seq_len=seq_len), grid=(batch, nq),
        in_specs=[pl.BlockSpec((QBLOCK, 512), lambda b, i: (rb(b, i), C_NQ // 512)),
                  pl.BlockSpec((QBLOCK, LANE), lambda b, i: (rb(b, i), C_NG // LANE)),
                  kv(C_NKV // LANE), kv(C_NKV // LANE + 1), kv(C_NKV // LANE + 2),
                  pl.BlockSpec((1, seq_len), lambda b, i: (0, 0))],
        out_specs=pl.BlockSpec((QBLOCK, 512), lambda b, i: (rb(b, i), 0)),
        out_shape=jax.ShapeDtypeStruct((batch * seq_len, 512), F32),
        scratch_shapes=[pltpu.VMEM((n_cmp, NSA_D), BF16), pltpu.VMEM((n_cmp, NSA_D), BF16)]
        + [pltpu.VMEM((seq_len, NSA_D), BF16)] * 4,
        compiler_params=_cp(("arbitrary", "arbitrary")), name="nsa_prompt")(z, z, z, z, z, lg)


def _mla_p_body(q_ref, k_ref, o_ref):
    i = pl.program_id(1)
    qb, tk = QBLOCK, 512
    rows = MLA_H * qb
    q2 = q_ref[...].reshape(rows, MLA_QW)
    tq = i * qb + (lax.broadcasted_iota(jnp.int32, (rows, 1), 0) & (qb - 1))

    def tile(j, carry, masked):
        m, l, acc = carry
        k0 = pl.multiple_of(j * tk, tk)
        kt = k_ref[pl.ds(k0, tk), :]
        s = _dot_nt(q2, kt)
        if masked:
            kpos = k0 + lax.broadcasted_iota(jnp.int32, (1, tk), 1)
            s = jnp.where(kpos <= tq, s, NEG)
        m_new = jnp.maximum(m, jnp.max(s, axis=-1, keepdims=True))
        alpha = jnp.exp(m - m_new)
        p = jnp.exp(s - m_new)
        l = alpha * l + jnp.sum(p, axis=-1, keepdims=True)
        acc = alpha * acc + _dot(p.astype(BF16), kt[:, :MLA_KVR])
        return m_new, l, acc

    n_full = (i * qb) // tk
    m0 = jnp.full((rows, 1), NEG, F32)
    carry = lax.fori_loop(0, n_full, functools.partial(tile, masked=False),
                          (m0, jnp.zeros((rows, 1), F32), jnp.zeros((rows, MLA_KVR), F32)))
    _, l, acc = tile(n_full, carry, True)
    o = (acc / l).astype(BF16)
    for h in range(MLA_H):
        o_ref[:, h * MLA_KVR:(h + 1) * MLA_KVR] = o[h * qb:(h + 1) * qb]


def mla_prompt(qm, km, batch, seq_len):
    nq = seq_len // QBLOCK
    return pl.pallas_call(
        _mla_p_body, grid=(batch, nq),
        in_specs=[pl.BlockSpec((MLA_H, QBLOCK, MLA_QW), lambda b, i: (0, b * nq + i, 0)),
                  pl.BlockSpec((seq_len, MLA_QW), lambda b, i: (b, 0))],
        out_specs=pl.BlockSpec((QBLOCK, MLA_H * MLA_KVR), lambda b, i: (b * nq + i, 0)),
        out_shape=jax.ShapeDtypeStruct((batch * seq_len, MLA_H * MLA_KVR), BF16),
        compiler_params=_cp(("arbitrary", "arbitrary")), name="mla_prompt")(qm, km)


CH_PAGES = 16
CH_ROWS = CH_PAGES * PAGE
HPAD = 16


def _chunk_copies(cache_ref, layer, pt_ref, buf_ref, sem_ref, seq, chunk, slot, feats):
    copies = []
    for p in range(CH_PAGES):
        page = pt_ref[seq, chunk * CH_PAGES + p]
        src = cache_ref.at[layer, page] if feats is None else cache_ref.at[layer, page, pl.ds(0, feats), :]
        copies.append(pltpu.make_async_copy(src, buf_ref.at[slot, :, pl.ds(p * PAGE, PAGE)], sem_ref.at[slot]))
    return copies


def _paged_pipeline(cache_ref, layer, pt_ref, buf_ref, sem_ref, n_chunks, feats, compute):
    s = pl.program_id(0)
    n_seq = pl.num_programs(0)

    @pl.when(s == 0)
    def _():
        for cp in _chunk_copies(cache_ref, layer, pt_ref, buf_ref, sem_ref, 0, 0, 0, feats):
            cp.start()

    def body(c, carry):
        slot = c % 2
        nxt = c + 1

        @pl.when(nxt < n_chunks)
        def _():
            for cp in _chunk_copies(cache_ref, layer, pt_ref, buf_ref, sem_ref, s, nxt, 1 - slot, feats):
                cp.start()

        @pl.when((nxt == n_chunks) & (s + 1 < n_seq))
        def _():
            for cp in _chunk_copies(cache_ref, layer, pt_ref, buf_ref, sem_ref, s + 1, 0, 1 - slot, feats):
                cp.start()

        for cp in _chunk_copies(cache_ref, layer, pt_ref, buf_ref, sem_ref, s, c, slot, feats):
            cp.wait()
        return compute(c, slot, carry)

    return body


def _nsa_cmp_body(pt_ref, q_ref, lg_ref, cache_ref, oc_ref, imp_ref, buf_ref, sem_ref, wm_ref, kv_ref,
                  *, layer, n_chunks, t_pos):
    s = pl.program_id(0)
    cmp_rows = CH_ROWS // CMP_BLOCK

    @pl.when(s == 0)
    def _():
        wc = _cmp_weights(lg_ref[...])
        blk = lax.broadcasted_iota(jnp.int32, (cmp_rows, CH_ROWS), 0)
        pos = lax.broadcasted_iota(jnp.int32, (cmp_rows, CH_ROWS), 1)
        wm_ref[...] = jnp.where((pos // CMP_BLOCK) == blk, wc, 0.0).astype(BF16)

    def compute(c, slot, carry):
        r0 = pl.multiple_of(c * cmp_rows, cmp_rows)
        kv_ref[pl.ds(r0, cmp_rows), :] = _dot_nt(wm_ref[...], buf_ref[slot].astype(BF16))
        return carry

    body = _paged_pipeline(cache_ref, layer, pt_ref, buf_ref, sem_ref, n_chunks, 2 * NSA_D, compute)
    lax.fori_loop(0, n_chunks, body, 0)

    n_cmp = n_chunks * cmp_rows
    kvcb = kv_ref[...]
    q2 = (q_ref[0] * NSA_SCALE).astype(BF16)
    hrow = lax.broadcasted_iota(jnp.int32, (HPAD, 1), 0)
    slope = _alibi_slope(jnp.minimum(hrow, NSA_H - 1))
    lane_c = lax.broadcasted_iota(jnp.int32, (1, n_cmp), 1)
    dist_c = (t_pos - (lane_c * CMP_BLOCK + (CMP_BLOCK - 1))).astype(F32)
    s_c = _dot_nt(q2, kvcb[:, :NSA_D].astype(BF16)) - slope * dist_c
    p_c = _softmax_rows(s_c, dist_c >= 0)
    oc_ref[0] = _dot(p_c.astype(BF16), kvcb[:, NSA_D:].astype(BF16))
    imp_ref[0] = jnp.sum(jnp.where(hrow < NSA_H, p_c, 0.0), axis=0, keepdims=True)


def nsa_decode_cmp(q16, page_table, cache4, lw, layer):
    n_seq, n_pages = page_table.shape
    n_chunks = n_pages // CH_PAGES
    n_cmp = n_pages * PAGE // CMP_BLOCK
    lg = jnp.tile(lw["nsa_cmp_logits"], CH_ROWS // CMP_BLOCK)[None]
    gs = pltpu.PrefetchScalarGridSpec(
        num_scalar_prefetch=1, grid=(n_seq,),
        in_specs=[pl.BlockSpec((1, HPAD, NSA_D), lambda s, pt: (s, 0, 0)),
                  pl.BlockSpec((1, CH_ROWS), lambda s, pt: (0, 0)),
                  pl.BlockSpec(memory_space=pl.ANY)],
        out_specs=[pl.BlockSpec((1, HPAD, NSA_D), lambda s, pt: (s, 0, 0)),
                   pl.BlockSpec((1, 1, n_cmp), lambda s, pt: (s, 0, 0))],
        scratch_shapes=[pltpu.VMEM((2, 2 * NSA_D, CH_ROWS), F32), pltpu.SemaphoreType.DMA((2,)),
                        pltpu.VMEM((CH_ROWS // CMP_BLOCK, CH_ROWS), BF16), pltpu.VMEM((n_cmp, LANE), F32)])
    return pl.pallas_call(
        functools.partial(_nsa_cmp_body, layer=layer, n_chunks=n_chunks, t_pos=n_pages * PAGE),
        grid_spec=gs,
        out_shape=[jax.ShapeDtypeStruct((n_seq, HPAD, NSA_D), F32), jax.ShapeDtypeStruct((n_seq, 1, n_cmp), F32)],
        compiler_params=_cp(("arbitrary",)), name="nsa_decode_cmp")(page_table, q16, lg, cache4)


def _nsa_topk_body(imp_ref, idx_ref, *, t_pos):
    imp = imp_ref[...]
    n_seq, n_cmp = imp.shape
    n_sel = -(-(t_pos + 1) // SEL_BLOCK)
    width = idx_ref.shape[-1]
    pr = lax.broadcasted_iota(jnp.int32, (n_cmp, width), 0)
    pc = lax.broadcasted_iota(jnp.int32, (n_cmp, width), 1)
    pair = ((pr // (SEL_BLOCK // CMP_BLOCK)) == pc).astype(F32)
    vals = _dot(imp, pair, HIGHEST)
    lane = lax.broadcasted_iota(jnp.int32, (1, width), 1)
    forced = (lane == t_pos // SEL_BLOCK) | (lane == 0)
    causal = (lane * SEL_BLOCK <= t_pos) & (lane < n_sel)
    vals = jnp.where(forced, jnp.inf, jnp.where(causal, vals, -jnp.inf))
    _, idxs, oks = _topk_mask(vals, lane, min(SEL_TOPK, n_sel))
    out = jnp.full((n_seq, width), -1, jnp.int32)
    for r, (idx, ok) in enumerate(zip(idxs, oks)):
        out = jnp.where(lane == r, jnp.where(ok, idx.astype(jnp.int32), -1), out)
    idx_ref[...] = out


def nsa_decode_topk(imp, t_pos):
    n_seq, n_cmp = imp.shape
    width = 384
    return pl.pallas_call(
        functools.partial(_nsa_topk_body, t_pos=t_pos), grid=(1,),
        in_specs=[pl.BlockSpec((n_seq, n_cmp), lambda i: (0, 0))],
        out_specs=pl.BlockSpec((n_seq, width), lambda i: (0, 0)),
        out_shape=jax.ShapeDtypeStruct((n_seq, width), jnp.int32),
        compiler_params=_cp(("arbitrary",)), name="nsa_decode_topk")(imp)


def _sel_copies(cache_ref, layer, pt_ref, idx_ref, buf_ref, sem_ref, seq, slot, n_blk):
    per_page = PAGE // SEL_BLOCK
    copies = []
    for r in range(SEL_TOPK):
        blk = jnp.clip(idx_ref[seq, r], 0, n_blk - 1)
        page = pt_ref[seq, blk // per_page]
        src = cache_ref.at[layer, page, pl.ds(2 * NSA_D, 2 * NSA_D), :]
        copies.append(pltpu.make_async_copy(src, buf_ref.at[slot, :, pl.ds(r * PAGE, PAGE)], sem_ref.at[slot]))
    return copies


def _attend_with_new(q2, slope, kt_b, vt_b, dist, ok, q_f32, k_new, v_new):
    s = jnp.where(ok, _dot(q2, kt_b) - slope * dist, NEG)
    s_new = jnp.sum(q_f32 * k_new, axis=-1, keepdims=True)
    m = jnp.maximum(jnp.max(s, axis=-1, keepdims=True), s_new)
    e = jnp.exp(s - m)
    e_new = jnp.exp(s_new - m)
    d = jnp.sum(e, axis=-1, keepdims=True) + e_new
    return (_dot_nt(e.astype(BF16), vt_b) + e_new * v_new) / d


def _nsa_sel_body(pt_ref, idx_ref, q_ref, g_ref, oc_ref, new_ref, win_ref, cache_ref, o_ref, buf_ref, sem_ref,
                  *, layer, t_pos):
    s = pl.program_id(0)
    n_seq = pl.num_programs(0)
    n_blk = t_pos // SEL_BLOCK
    slot = s % 2
    args = (cache_ref, layer, pt_ref, idx_ref, buf_ref, sem_ref)

    @pl.when(s == 0)
    def _():
        for cp in _sel_copies(*args, 0, 0, n_blk):
            cp.start()

    @pl.when(s + 1 < n_seq)
    def _():
        for cp in _sel_copies(*args, s + 1, 1 - slot, n_blk):
            cp.start()

    for cp in _sel_copies(*args, s, slot, n_blk):
        cp.wait()

    q = q_ref[0] * NSA_SCALE
    q2 = q.astype(BF16)
    hrow = lax.broadcasted_iota(jnp.int32, (HPAD, 1), 0)
    slope = _alibi_slope(jnp.minimum(hrow, NSA_H - 1))
    new = new_ref[0]
    per_page = PAGE // SEL_BLOCK
    nk = SEL_TOPK * PAGE
    lane = lax.broadcasted_iota(jnp.int32, (1, nk), 1)
    blk_of = jnp.zeros((1, nk), jnp.int32)
    for r in range(SEL_TOPK):
        blk_of = jnp.where(lane // PAGE == r, idx_ref[s, r], blk_of)
    in_page = lane % PAGE
    pos = (blk_of // per_page) * PAGE + in_page
    ok = (blk_of >= 0) & (blk_of < n_blk) & (in_page // SEL_BLOCK == blk_of % per_page)
    kv = buf_ref[slot]
    o_s = _attend_with_new(q2, slope, kv[:NSA_D].astype(BF16), kv[NSA_D:].astype(BF16),
                           (t_pos - pos).astype(F32), ok, q, new[:, 2 * NSA_D:3 * NSA_D], new[:, 3 * NSA_D:4 * NSA_D])
    wkv = win_ref[0]
    wbuf = wkv.shape[1]
    lane_w = lax.broadcasted_iota(jnp.int32, (1, wbuf), 1)
    dist_w = wbuf - lane_w
    ok_w = dist_w <= WINDOW
    o_w = _attend_with_new(q2, slope, wkv[:NSA_D].astype(BF16), wkv[NSA_D:].astype(BF16),
                           dist_w.astype(F32), ok_w, q, new[:, 4 * NSA_D:5 * NSA_D], new[:, 5 * NSA_D:6 * NSA_D])
    gi = _sigmoid(g_ref[0])
    o_ref[0] = gi[:, 0:1] * oc_ref[0] + gi[:, 1:2] * o_s + gi[:, 2:3] * o_w


def nsa_decode_sel(q16, g16, o_c, new_rows, win, page_table, idx, cache4, layer):
    n_seq, n_pages = page_table.shape
    wbuf = win.shape[2]
    gs = pltpu.PrefetchScalarGridSpec(
        num_scalar_prefetch=2, grid=(n_seq,),
        in_specs=[pl.BlockSpec((1, HPAD, NSA_D), lambda s, pt, ix: (s, 0, 0)),
                  pl.BlockSpec((1, HPAD, 3), lambda s, pt, ix: (s, 0, 0)),
                  pl.BlockSpec((1, HPAD, NSA_D), lambda s, pt, ix: (s, 0, 0)),
                  pl.BlockSpec((1, 1, 512), lambda s, pt, ix: (s, 0, 0)),
                  pl.BlockSpec((1, 2 * NSA_D, wbuf), lambda s, pt, ix: (layer * n_seq + s, 0, 0)),
                  pl.BlockSpec(memory_space=pl.ANY)],
        out_specs=pl.BlockSpec((1, HPAD, NSA_D), lambda s, pt, ix: (s, 0, 0)),
        scratch_shapes=[pltpu.VMEM((2, 2 * NSA_D, SEL_TOPK * PAGE), F32), pltpu.SemaphoreType.DMA((2,))])
    return pl.pallas_call(
        functools.partial(_nsa_sel_body, layer=layer, t_pos=n_pages * PAGE),
        grid_spec=gs, out_shape=jax.ShapeDtypeStruct((n_seq, HPAD, NSA_D), F32),
        compiler_params=_cp(("arbitrary",)), name="nsa_decode_sel")(
            page_table, idx, q16, g16, o_c, new_rows, win, cache4)


def _mla_s_body(pt_ref, q_ref, new_ref, cache_ref, o_ref, buf_ref, sem_ref, *, layer, n_chunks):
    q = q_ref[0]
    width = MLA_KVR + MLA_DR
    qk = q[:, :width]

    def compute(c, slot, carry):
        m, l, acc = carry
        kt = buf_ref[slot].astype(BF16)
        s = _dot(qk, kt)
        m_new = jnp.maximum(m, jnp.max(s, axis=-1, keepdims=True))
        alpha = jnp.exp(m - m_new)
        p = jnp.exp(s - m_new)
        l = alpha * l + jnp.sum(p, axis=-1, keepdims=True)
        acc = alpha * acc + _dot_nt(p.astype(BF16), kt[:MLA_KVR])
        return m_new, l, acc

    body = _paged_pipeline(cache_ref, layer, pt_ref, buf_ref, sem_ref, n_chunks, None, compute)
    m, l, acc = lax.fori_loop(0, n_chunks, body, (jnp.full((HPAD, 1), NEG, F32), jnp.zeros((HPAD, 1), F32),
                                                  jnp.zeros((HPAD, MLA_KVR), F32)))
    k_new = new_ref[0].astype(F32)
    s_new = jnp.sum(q.astype(F32) * k_new, axis=-1, keepdims=True)
    m_new = jnp.maximum(m, s_new)
    alpha = jnp.exp(m - m_new)
    e_new = jnp.exp(s_new - m_new)
    o_ref[0] = (alpha * acc + e_new * k_new[:, :MLA_KVR]) / (alpha * l + e_new)


def mla_decode(q16, k_new, page_table, cache_mla, layer):
    n_seq, n_pages = page_table.shape
    width = MLA_KVR + MLA_DR
    gs = pltpu.PrefetchScalarGridSpec(
        num_scalar_prefetch=1, grid=(n_seq,),
        in_specs=[pl.BlockSpec((1, HPAD, MLA_QW), lambda s, pt: (s, 0, 0)),
                  pl.BlockSpec((1, 1, MLA_QW), lambda s, pt: (s, 0, 0)),
                  pl.BlockSpec(memory_space=pl.ANY)],
        out_specs=pl.BlockSpec((1, HPAD, MLA_KVR), lambda s, pt: (s, 0, 0)),
        scratch_shapes=[pltpu.VMEM((2, width, CH_ROWS), F32), pltpu.SemaphoreType.DMA((2,))])
    return pl.pallas_call(
        functools.partial(_mla_s_body, layer=layer, n_chunks=n_pages // CH_PAGES),
        grid_spec=gs, out_shape=jax.ShapeDtypeStruct((n_seq, HPAD, MLA_KVR), F32),
        compiler_params=_cp(("arbitrary",)), name="mla_decode")(page_table, q16, k_new, cache_mla)


def _merge_body(gla_ref, nsa_ref, rg_ref, lat_ref, mg_ref, x_ref, wuv_ref, wb_ref, wo_ref, g_ref, b_ref, o_ref):
    o_mla = _dot(lat_ref[...], wuv_ref[...])
    branches = (gla_ref[...], nsa_ref[...], rg_ref[...], o_mla)
    merged = None
    for n, br in enumerate(branches):
        proj = _dot(br.astype(BF16), wb_ref[n])
        term = _sigmoid(mg_ref[:, n * D_MODEL:(n + 1) * D_MODEL]) * proj
        merged = term if merged is None else merged + term
    y = DN_ALPHA * x_ref[...] + _dot(merged.astype(BF16), wo_ref[...])
    o_ref[...] = _layer_norm(y, g_ref[...], b_ref[...])


def merge(o_gla, o_nsa, o_rg, o_lat, z, x, lw):
    n = x.shape[0]
    tm = min(n, 256)
    row = lambda w: pl.BlockSpec((tm, w), lambda i: (i, 0))
    full = lambda a: pl.BlockSpec(a.shape, lambda i: (0,) * a.ndim)
    ws = (lw["w_uv_bd"], lw["w_branch"], lw["w_out"], lw["ln1_g"], lw["ln1_b"])
    return pl.pallas_call(
        _merge_body, grid=(n // tm,),
        in_specs=[row(512), row(512), row(512), row(MLA_H * MLA_KVR), row(N_BRANCH * D_MODEL), row(D_MODEL)]
        + [full(a) for a in ws],
        out_specs=row(D_MODEL), out_shape=jax.ShapeDtypeStruct((n, D_MODEL), F32),
        compiler_params=_cp(("parallel",)), name="merge")(o_gla, o_nsa, o_rg, o_lat, z, x, *ws)


def _ffn_body(x_ref, wu_ref, wd_ref, g_ref, b_ref, o_ref, xb_ref, acc_ref):
    f = pl.program_id(1)

    @pl.when(f == 0)
    def _():
        xb_ref[...] = x_ref[...].astype(BF16)
        acc_ref[...] = jnp.zeros_like(acc_ref)

    h = jnp.maximum(_dot(xb_ref[...], wu_ref[...]), 0.0)
    acc_ref[...] += _dot((h * h).astype(BF16), wd_ref[...])

    @pl.when(f == pl.num_programs(1) - 1)
    def _():
        o_ref[...] = _layer_norm(DN_ALPHA * x_ref[...] + acc_ref[...], g_ref[...], b_ref[...])


def ffn(x, lw):
    n = x.shape[0]
    tm, tf = min(n, 1024), 1024
    vec = pl.BlockSpec((1, D_MODEL), lambda i, f: (0, 0))
    return pl.pallas_call(
        _ffn_body, grid=(n // tm, D_FF // tf),
        in_specs=[pl.BlockSpec((tm, D_MODEL), lambda i, f: (i, 0)),
                  pl.BlockSpec((D_MODEL, tf), lambda i, f: (0, f)),
                  pl.BlockSpec((tf, D_MODEL), lambda i, f: (f, 0)), vec, vec],
        out_specs=pl.BlockSpec((tm, D_MODEL), lambda i, f: (i, 0)),
        out_shape=jax.ShapeDtypeStruct((n, D_MODEL), F32),
        scratch_shapes=[pltpu.VMEM((tm, D_MODEL), BF16), pltpu.VMEM((tm, D_MODEL), F32)],
        compiler_params=_cp(("parallel", "arbitrary")), name="ffn")(x, lw["w_up"], lw["w_down"], lw["ln2_g"],
                                                                      lw["ln2_b"])


def _pad_cols(a, width):
    return jnp.pad(a, ((0, 0), (0, width - a.shape[1])))


def _block_diag(blocks):
    n, r, c = blocks.shape
    eye = jnp.eye(n, dtype=blocks.dtype)
    return (eye[:, None, :, None] * blocks[:, :, None, :]).reshape(n * r, n * c)


def _layer_weights(l, w_in, gla_w_a2, gla_b_a, gla_norm_g, nsa_cmp_logits, rg_conv_w, rg_conv_b, rg_w_r, rg_b_r,
                   rg_w_i, rg_b_i, rg_lambda, mla_q_g, mla_w_uq, mla_kv_g, mla_w_uk, mla_w_uv, w_branch, w_out,
                   ln1_g, ln1_b, w_up, w_down, ln2_g, ln2_b):
    (gq, gk, gv, glr, gr, nq, nkv, ng, rx, rgt, mq, mkv, mkr, mg) = jnp.split(w_in[l], IN_SPLITS, axis=1)
    w_in_p = jnp.concatenate(
        [mg, gv, gr, nq, rx, rgt, _pad_cols(nkv, 512), _pad_cols(mq, 512), gq, gk, mkv,
         _pad_cols(glr, LANE), _pad_cols(ng, LANE), _pad_cols(mkr, LANE)], axis=1)
    w_in_p = _pad_cols(w_in_p, DZ).astype(BF16)
    uq = mla_w_uq[l].reshape(MLA_QR, MLA_H, MLA_DN + MLA_DR)
    uq_nope = uq[:, :, :MLA_DN].reshape(MLA_QR, MLA_H * MLA_DN)
    uq_rope = jnp.pad(uq[:, :, MLA_DN:], ((0, 0), (0, 0), (0, LANE - MLA_DR))).reshape(MLA_QR, MLA_H * LANE)
    pad_rows = lambda a: jnp.pad(a, ((0, 512 - MLA_QR), (0, 0)))
    a2p = jnp.pad(gla_w_a2[l], ((0, LANE - GLA_RANK), (0, 0)))
    return {
        "w_in_p": w_in_p,
        "gla_w_a2p": a2p,
        "gla_b_a": gla_b_a[l][None],
        "gla_norm_g": gla_norm_g[l].reshape(1, GLA_H * GLA_DV),
        "gla_w_a2h": a2p.reshape(LANE, GLA_H, GLA_DK).transpose(1, 0, 2),
        "gla_b_ah": gla_b_a[l].reshape(GLA_H, 1, GLA_DK),
        "gla_norm_gh": gla_norm_g[l].reshape(GLA_H, 1, GLA_DV),
        "nsa_cmp_logits": nsa_cmp_logits[l],
        "rg_conv_w": rg_conv_w[l], "rg_conv_b": rg_conv_b[l][None],
        "rg_w_r_bd": _block_diag(rg_w_r[l]).astype(BF16), "rg_b_r": rg_b_r[l][None],
        "rg_w_i_bd": _block_diag(rg_w_i[l]).astype(BF16), "rg_b_i": rg_b_i[l][None],
        "rg_lambda": rg_lambda[l][None],
        "mla_q_g": _pad_cols(mla_q_g[l][None], 512),
        "w_uq_nope": pad_rows(uq_nope).astype(BF16),
        "w_uq_rope": pad_rows(uq_rope).astype(BF16),
        "w_uk_bd": _block_diag(mla_w_uk[l].transpose(1, 2, 0)).astype(BF16),
        "mla_kv_g": mla_kv_g[l][None],
        "w_uv_bd": _block_diag(mla_w_uv[l].transpose(1, 0, 2)).astype(BF16),
        "w_branch": w_branch[l].astype(BF16), "w_out": w_out[l].astype(BF16),
        "ln1_g": ln1_g[l][None], "ln1_b": ln1_b[l][None],
        "w_up": w_up[l].astype(BF16), "w_down": w_down[l].astype(BF16),
        "ln2_g": ln2_g[l][None], "ln2_b": ln2_b[l][None],
    }


def _rope_tables(pos):
    half = MLA_DR // 2
    freq = ROPE_BASE ** (-jnp.arange(half, dtype=F32) / half)
    ang = pos.astype(F32)[:, None] * freq
    cos, sin = jnp.cos(ang), jnp.sin(ang)
    zero = jnp.zeros_like(cos)
    pad = lambda a, b: jnp.pad(jnp.concatenate([a, b], axis=1), ((0, 0), (0, LANE - MLA_DR)))
    return pad(cos, cos), pad(-sin, zero), pad(zero, sin)


def _prompt_layer(x, lw, tabs, batch, seq_len):
    z = in_proj(x, lw["w_in_p"])
    qm, mla_rows, km = attn_prep(z, tabs, lw, seq_len)
    o_gla, s_gla = gla_prompt(z, lw, batch, seq_len)
    o_rg, rg_h, rg_buf = rglru_prompt(z, lw, batch, seq_len)
    o_nsa = nsa_prompt(z, lw, batch, seq_len)
    o_lat = mla_prompt(qm, km, batch, seq_len)
    x = ffn(merge(o_gla, o_nsa, o_rg, o_lat, z, x, lw), lw)
    nkv = z[:, C_NKV:C_NKV + 6 * NSA_D].reshape(batch, seq_len, 6, NSA_D)
    keep = min(WINDOW, seq_len)
    states = (nkv[:, :, :4], nkv[:, seq_len - keep:, 4:], mla_rows.reshape(batch, seq_len, -1), s_gla,
              rg_h.reshape(batch, RG_W), rg_buf)
    return x, states


def _sample_layer(x, lw, tabs, layer, cache4, win3, win_l, cache_mla, page_table, s_gla, rg_buf, rg_h):
    n_seq = x.shape[0]
    t_pos = page_table.shape[1] * PAGE
    z = in_proj(x, lw["w_in_p"])
    qm, mla_rows, km = attn_prep(z, tabs, lw, 1)
    o_gla, s_gla_new = gla_decode(z, s_gla, lw)
    o_rg, rg_buf_new, rg_h_new = rglru_decode(z, rg_buf, rg_h, lw)
    pad_h = lambda a: jnp.pad(a, ((0, 0), (0, HPAD - a.shape[1]), (0, 0)))
    q16 = pad_h(z[:, C_NQ:C_NQ + NSA_H * NSA_D].reshape(n_seq, NSA_H, NSA_D))
    g16 = pad_h(z[:, C_NG:C_NG + 3 * NSA_H].reshape(n_seq, NSA_H, 3))
    o_c, imp = nsa_decode_cmp(q16, page_table, cache4, lw, layer)
    idx = nsa_decode_topk(imp.reshape(n_seq, -1), t_pos)[:, :SEL_TOPK]
    new_rows = z[:, None, C_NKV:C_NKV + 512]
    o_nsa = nsa_decode_sel(q16, g16, o_c, new_rows, win3, page_table, idx, cache4, layer)
    o_nsa = o_nsa[:, :NSA_H].reshape(n_seq, NSA_H * NSA_D)
    qm16 = pad_h(qm.transpose(1, 0, 2))
    o_lat = mla_decode(qm16, km[:, None], page_table, cache_mla, layer)
    o_lat = o_lat[:, :MLA_H].reshape(n_seq, MLA_H * MLA_KVR).astype(BF16)
    x = ffn(merge(o_gla, o_nsa, o_rg, o_lat, z, x, lw), lw)
    nkv = z[:, C_NKV:C_NKV + 6 * NSA_D].reshape(n_seq, 1, 6, NSA_D)
    wkv = jnp.concatenate([win_l, nkv[:, :, 4:]], axis=1)
    keep = min(WINDOW, wkv.shape[1])
    states = (nkv[:, :, :4], wkv[:, wkv.shape[1] - keep:], mla_rows.reshape(n_seq, 1, -1), s_gla_new, rg_h_new,
              rg_buf_new)
    return x, states


def kernel(x_prompt, x_sample, cache_nsa, cache_nsa_win, cache_mla, state_gla, state_rg_h, state_rg_conv,
           page_table, w_in, gla_w_a2, gla_b_a, gla_norm_g, nsa_cmp_logits, rg_conv_w, rg_conv_b, rg_w_r,
           rg_b_r, rg_w_i, rg_b_i, rg_lambda, mla_q_g, mla_w_uq, mla_kv_g, mla_w_uk, mla_w_uv, w_branch,
           w_out, ln1_g, ln1_b, w_up, w_down, ln2_g, ln2_b):
    batch, seq_len, _ = x_prompt.shape
    n_seq, dec_seq, _ = x_sample.shape
    assert dec_seq == 1 and seq_len % 512 == 0 and page_table.shape[1] % (2 * CH_PAGES) == 0
    past = page_table.shape[1] * PAGE
    depth = w_in.shape[0]
    params = (w_in, gla_w_a2, gla_b_a, gla_norm_g, nsa_cmp_logits, rg_conv_w, rg_conv_b, rg_w_r, rg_b_r, rg_w_i,
              rg_b_i, rg_lambda, mla_q_g, mla_w_uq, mla_kv_g, mla_w_uk, mla_w_uv, w_branch, w_out, ln1_g, ln1_b,
              w_up, w_down, ln2_g, ln2_b)
    tabs_p = _rope_tables(jnp.arange(seq_len, dtype=jnp.int32))
    tabs_s = _rope_tables(jnp.full((n_seq,), past, jnp.int32))
    n_phys = cache_nsa.shape[1]
    cache4 = cache_nsa.transpose(0, 1, 3, 4, 2).reshape(depth, n_phys, 4 * NSA_D, PAGE)
    cache_mla = cache_mla.transpose(0, 1, 3, 2)
    wbuf = cache_nsa_win.shape[2]
    win3 = cache_nsa_win.transpose(0, 1, 3, 4, 2).reshape(depth * n_seq, 2 * NSA_D, wbuf)
    yp = x_prompt.reshape(batch * seq_len, D_MODEL)
    ys = x_sample.reshape(n_seq, D_MODEL)
    st_p, st_s = [], []
    for l in range(depth):
        lw = _layer_weights(l, *params)
        yp, sp = _prompt_layer(yp, lw, tabs_p, batch, seq_len)
        ys, ss = _sample_layer(ys, lw, tabs_s, l, cache4, win3, cache_nsa_win[l], cache_mla, page_table,
                               state_gla[l], state_rg_conv[l], state_rg_h[l])
        st_p.append(sp)
        st_s.append(ss)
    stack = lambda sts, i: jnp.stack([s[i] for s in sts])
    return (yp.reshape(batch, seq_len, D_MODEL), ys.reshape(n_seq, 1, D_MODEL),
            stack(st_p, 0), stack(st_s, 0), stack(st_p, 1), stack(st_s, 1), stack(st_p, 2), stack(st_s, 2),
            stack(st_p, 3), stack(st_s, 3), stack(st_p, 4), stack(st_s, 4), stack(st_p, 5), stack(st_s, 5))
```

```python
import functools

import numpy as np
import jax
import jax.numpy as jnp
from jax import lax
from jax.experimental import pallas as pl
from jax.experimental.pallas import tpu as pltpu

F32 = jnp.float32
BF16 = jnp.bfloat16
HIGHEST = lax.Precision.HIGHEST

D_MODEL = 1024
DEPTH = 2
PAGE = 128
GLA_H, GLA_DK, GLA_DV, GLA_RANK, GLA_TAU, GLA_CHUNK = 4, 64, 128, 16, 16.0, 64
NSA_H, NSA_D = 8, 64
NSA_SCALE = NSA_D ** -0.5
CMP_BLOCK, SEL_BLOCK, SEL_TOPK, WINDOW, QBLOCK = 32, 64, 16, 512, 128
RG_W, RG_BLOCKS, CONV_W, RG_C = 512, 8, 4, 8.0
RG_BD = RG_W // RG_BLOCKS
MLA_H, MLA_QR, MLA_KVR, MLA_DN, MLA_DR, MLA_DV = 8, 384, 256, 64, 32, 64
MLA_SCALE = (MLA_DN + MLA_DR) ** -0.5
ROPE_BASE = 10000.0
N_BRANCH, BRANCH_W = 4, 512
D_FF = 4 * D_MODEL
DN_ALPHA = (2 * DEPTH) ** 0.25
LN_EPS = 1e-5
IN_SIZES = (GLA_H * GLA_DK, GLA_H * GLA_DK, GLA_H * GLA_DV, GLA_RANK, GLA_H * GLA_DV,
            NSA_H * NSA_D, 6 * NSA_D, 3 * NSA_H, RG_W, RG_W, MLA_QR, MLA_KVR, MLA_DR, N_BRANCH * D_MODEL)
IN_SPLITS = tuple(int(v) for v in np.cumsum(IN_SIZES)[:-1])

LANE = 128
VMEM_LIMIT = 56 * 2 ** 20
NEG = -1e30
MLA_QW = MLA_KVR + LANE

C_MG, C_GV, C_GR, C_NQ, C_RX, C_RGT, C_NKV, C_MQ = 0, 4096, 4608, 5120, 5632, 6144, 6656, 7168
C_GQ, C_GK, C_MKV, C_GLR, C_NG, C_MKR = 7680, 7936, 8192, 8448, 8576, 8704
DZ = 9216


def _cp(sem, vmem=VMEM_LIMIT):
    return pltpu.CompilerParams(dimension_semantics=sem, vmem_limit_bytes=vmem)


def _dot(a, b, precision=None):
    return jnp.dot(a, b, preferred_element_type=F32, precision=precision)


def _dot_nt(a, b):
    return lax.dot_general(a, b, (((1,), (1,)), ((), ())), preferred_element_type=F32)


def _dot_tn(a, b, precision=None):
    return lax.dot_general(a, b, (((0,), (0,)), ((), ())), preferred_element_type=F32, precision=precision)


def _log_sigmoid(x):
    return jnp.minimum(x, 0.0) - jnp.log(1.0 + jnp.exp(-jnp.abs(x)))


def _sigmoid(x):
    return 1.0 / (1.0 + jnp.exp(-x))


def _layer_norm(x, g, b):
    mu = jnp.mean(x, axis=-1, keepdims=True)
    xc = x - mu
    var = jnp.mean(xc * xc, axis=-1, keepdims=True)
    return xc * lax.rsqrt(var + LN_EPS) * g + b


def _alibi_slope(h):
    return lax.bitcast_convert_type((126 - h) << 23, F32)


def _softmax_rows(s, valid):
    s = jnp.where(valid, s, -jnp.inf)
    m = jnp.max(s, axis=-1, keepdims=True)
    m = jnp.where(m > -jnp.inf, m, 0.0)
    e = jnp.exp(s - m)
    d = jnp.sum(e, axis=-1, keepdims=True)
    return e / jnp.where(d > 0, d, 1.0)


def _inproj_body(x_ref, w_ref, o_ref, xb_ref):
    @pl.when(pl.program_id(1) == 0)
    def _():
        xb_ref[...] = x_ref[...].astype(BF16)

    o_ref[...] = _dot(xb_ref[...], w_ref[...])


def in_proj(x, w):
    n = x.shape[0]
    tm, tn = min(n, 2048), 1024
    return pl.pallas_call(
        _inproj_body, grid=(n // tm, DZ // tn),
        in_specs=[pl.BlockSpec((tm, D_MODEL), lambda i, j: (i, 0)),
                  pl.BlockSpec((D_MODEL, tn), lambda i, j: (0, j))],
        out_specs=pl.BlockSpec((tm, tn), lambda i, j: (i, j)),
        out_shape=jax.ShapeDtypeStruct((n, DZ), F32),
        scratch_shapes=[pltpu.VMEM((tm, D_MODEL), BF16)],
        compiler_params=_cp(("parallel", "arbitrary")), name="in_proj")(x, w)


def _rope_lanes(x, cos, sa, sb):
    w = x.shape[-1]
    return x * cos + pltpu.roll(x, w - 16, 1) * sa + pltpu.roll(x, 16, 1) * sb


def _prep_body(mq_ref, mkv_ref, mkr_ref, cos_ref, sa_ref, sb_ref, qg_ref, wn_ref, wr_ref, wuk_ref, kvg_ref,
               qm_ref, rows_ref, km_ref):
    mq = mq_ref[...]
    qn = mq * lax.rsqrt(jnp.sum(mq * mq, axis=-1, keepdims=True) * (1.0 / MLA_QR) + 1e-6) * qg_ref[...]
    qb = qn.astype(BF16)
    nope = _dot(qb, wn_ref[...])
    rq = _dot(qb, wr_ref[...])
    qlat = _dot(nope.astype(BF16), wuk_ref[...])
    cos, sa, sb = cos_ref[...], sa_ref[...], sb_ref[...]
    rq = _rope_lanes(rq, jnp.concatenate([cos] * MLA_H, axis=1), jnp.concatenate([sa] * MLA_H, axis=1),
                     jnp.concatenate([sb] * MLA_H, axis=1))
    for h in range(MLA_H):
        qh = jnp.concatenate([qlat[:, h * MLA_KVR:(h + 1) * MLA_KVR], rq[:, h * LANE:(h + 1) * LANE]], axis=1)
        qm_ref[h] = (qh * MLA_SCALE).astype(BF16)
    mkv = mkv_ref[...]
    ckv = mkv * lax.rsqrt(jnp.mean(mkv * mkv, axis=-1, keepdims=True) + 1e-6) * kvg_ref[...]
    kr = _rope_lanes(mkr_ref[...], cos, sa, sb)
    rows_ref[...] = jnp.concatenate([ckv, kr[:, :MLA_DR]], axis=1)
    km_ref[...] = jnp.concatenate([ckv, kr], axis=1).astype(BF16)


def attn_prep(z, tabs, lw, seq_len):
    n = z.shape[0]
    tm = min(n, 512)
    nt = seq_len // tm if seq_len >= tm else 1
    tab_spec = pl.BlockSpec((tm, LANE), lambda i: (i % nt, 0))
    full = lambda a: pl.BlockSpec(a.shape, lambda i: (0,) * a.ndim)
    ws = (lw["mla_q_g"], lw["w_uq_nope"], lw["w_uq_rope"], lw["w_uk_bd"], lw["mla_kv_g"])
    return pl.pallas_call(
        _prep_body, grid=(n // tm,),
        in_specs=[pl.BlockSpec((tm, 512), lambda i: (i, C_MQ // 512)),
                  pl.BlockSpec((tm, 256), lambda i: (i, C_MKV // 256)),
                  pl.BlockSpec((tm, LANE), lambda i: (i, C_MKR // LANE)),
                  tab_spec, tab_spec, tab_spec] + [full(a) for a in ws],
        out_specs=[pl.BlockSpec((MLA_H, tm, MLA_QW), lambda i: (0, i, 0)),
                   pl.BlockSpec((tm, MLA_KVR + MLA_DR), lambda i: (i, 0)),
                   pl.BlockSpec((tm, MLA_QW), lambda i: (i, 0))],
        out_shape=[jax.ShapeDtypeStruct((MLA_H, n, MLA_QW), BF16),
                   jax.ShapeDtypeStruct((n, MLA_KVR + MLA_DR), F32),
                   jax.ShapeDtypeStruct((n, MLA_QW), BF16)],
        compiler_params=_cp(("parallel",)), name="attn_prep")(z, z, z, *tabs, *ws)


def _gla_out(o, gr, ng):
    outs = []
    for h in range(GLA_H):
        oh = o[:, h * GLA_DV:(h + 1) * GLA_DV]
        outs.append(oh * lax.rsqrt(jnp.mean(oh * oh, axis=-1, keepdims=True) + 1e-6))
    y = jnp.concatenate(outs, axis=1) * ng
    return y * (gr * _sigmoid(gr))


def _gla_p_body(q_ref, k_ref, v_ref, gr_ref, glr_ref, wa_ref, ba_ref, ng_ref, o_ref, st_ref, s_ref, *, tile):
    t = pl.program_id(1)
    c = GLA_CHUNK

    @pl.when(t == 0)
    def _():
        s_ref[...] = jnp.zeros_like(s_ref)

    la = _log_sigmoid(_dot(glr_ref[...], wa_ref[...], HIGHEST) + ba_ref[...]) * (1.0 / GLA_TAU)
    row = lax.broadcasted_iota(jnp.int32, (c, c), 0)
    col = lax.broadcasted_iota(jnp.int32, (c, c), 1)
    tri = (row >= col).astype(F32)
    ones = jnp.ones((c, GLA_DV), F32)
    for ci in range(tile // c):
        rows = slice(ci * c, (ci + 1) * c)
        g = la[rows]
        b = _dot(tri, g, HIGHEST)
        blast_rep = _dot_tn(g, ones, HIGHEST)
        q = q_ref[rows, :] * (GLA_DK ** -0.5)
        k = k_ref[rows, :]
        v = v_ref[rows, :]
        outs = []
        for h in range(GLA_H):
            dk = slice(h * GLA_DK, (h + 1) * GLA_DK)
            bh, qh, kh = b[:, dk], q[:, dk], k[:, dk]
            vh = v[:, h * GLA_DV:(h + 1) * GLA_DV].astype(BF16)
            bmid = bh[c // 2 - 1:c // 2]
            blast = bh[c - 1:c]
            s_old = s_ref[h]
            o_inter = _dot((qh * jnp.exp(bh)).astype(BF16), s_old.astype(BF16))
            att = _dot_nt((qh * jnp.exp(bh - bmid)).astype(BF16), (kh * jnp.exp(bmid - bh)).astype(BF16)) * tri
            o_intra = _dot(att.astype(BF16), vh)
            kl = (kh * jnp.exp(blast - bh)).astype(BF16)
            s_ref[h] = jnp.exp(blast_rep[dk]) * s_old + _dot_tn(kl, vh)
            outs.append(o_inter + o_intra)
        o_ref[rows, :] = _gla_out(jnp.concatenate(outs, axis=1), gr_ref[rows, :], ng_ref[...])

    @pl.when(t == pl.num_programs(1) - 1)
    def _():
        st_ref[0] = s_ref[...]


def gla_prompt(z, lw, batch, seq_len):
    tile = 256
    nt = seq_len // tile
    rb = lambda b, t: b * nt + t
    full = lambda a: pl.BlockSpec(a.shape, lambda b, t: (0,) * a.ndim)
    ws = (lw["gla_w_a2p"], lw["gla_b_a"], lw["gla_norm_g"])
    return pl.pallas_call(
        functools.partial(_gla_p_body, tile=tile), grid=(batch, nt),
        in_specs=[pl.BlockSpec((tile, 256), lambda b, t: (rb(b, t), C_GQ // 256)),
                  pl.BlockSpec((tile, 256), lambda b, t: (rb(b, t), C_GK // 256)),
                  pl.BlockSpec((tile, 512), lambda b, t: (rb(b, t), C_GV // 512)),
                  pl.BlockSpec((tile, 512), lambda b, t: (rb(b, t), C_GR // 512)),
                  pl.BlockSpec((tile, LANE), lambda b, t: (rb(b, t), C_GLR // LANE))] + [full(a) for a in ws],
        out_specs=[pl.BlockSpec((tile, 512), lambda b, t: (rb(b, t), 0)),
                   pl.BlockSpec((1, GLA_H, GLA_DK, GLA_DV), lambda b, t: (b, 0, 0, 0))],
        out_shape=[jax.ShapeDtypeStruct((batch * seq_len, 512), F32),
                   jax.ShapeDtypeStruct((batch, GLA_H, GLA_DK, GLA_DV), F32)],
        scratch_shapes=[pltpu.VMEM((GLA_H, GLA_DK, GLA_DV), F32)],
        compiler_params=_cp(("arbitrary", "arbitrary")), name="gla_prompt")(z, z, z, z, z, *ws)


def _gla_s_body(q_ref, k_ref, v_ref, gr_ref, glr_ref, wa_ref, ba_ref, ng_ref, st_ref, o_ref, sto_ref):
    la = _log_sigmoid(_dot(glr_ref[...], wa_ref[0], HIGHEST) + ba_ref[0]) * (1.0 / GLA_TAU)
    ea = jnp.exp(la)
    q = q_ref[0] * (GLA_DK ** -0.5)
    k = k_ref[0]
    v = v_ref[...]
    o = jnp.zeros(v.shape, F32)
    for d in range(GLA_DK):
        s_new = ea[:, d:d + 1] * st_ref[:, d, :] + k[:, d:d + 1] * v
        sto_ref[:, d, :] = s_new
        o = o + q[:, d:d + 1] * s_new
    y = o * lax.rsqrt(jnp.mean(o * o, axis=-1, keepdims=True) + 1e-6) * ng_ref[0]
    gr = gr_ref[...]
    o_ref[...] = y * (gr * _sigmoid(gr))


def gla_decode(z, state, lw):
    s = z.shape[0]
    heads = lambda a: a.reshape(s, GLA_H, GLA_DK).transpose(1, 0, 2)
    qh = heads(z[:, C_GQ:C_GQ + 256])
    kh = heads(z[:, C_GK:C_GK + 256])
    st = state.reshape(s, GLA_H * GLA_DK, GLA_DV)
    o, st_new = pl.pallas_call(
        _gla_s_body, grid=(GLA_H,),
        in_specs=[pl.BlockSpec((1, s, GLA_DK), lambda h: (h, 0, 0)),
                  pl.BlockSpec((1, s, GLA_DK), lambda h: (h, 0, 0)),
                  pl.BlockSpec((s, GLA_DV), lambda h: (0, C_GV // GLA_DV + h)),
                  pl.BlockSpec((s, GLA_DV), lambda h: (0, C_GR // GLA_DV + h)),
                  pl.BlockSpec((s, LANE), lambda h: (0, C_GLR // LANE)),
                  pl.BlockSpec((1, LANE, GLA_DK), lambda h: (h, 0, 0)),
                  pl.BlockSpec((1, 1, GLA_DK), lambda h: (h, 0, 0)),
                  pl.BlockSpec((1, 1, GLA_DV), lambda h: (h, 0, 0)),
                  pl.BlockSpec((s, GLA_DK, GLA_DV), lambda h: (0, h, 0))],
        out_specs=[pl.BlockSpec((s, GLA_DV), lambda h: (0, h)),
                   pl.BlockSpec((s, GLA_DK, GLA_DV), lambda h: (0, h, 0))],
        out_shape=[jax.ShapeDtypeStruct((s, GLA_H * GLA_DV), F32),
                   jax.ShapeDtypeStruct(st.shape, F32)],
        compiler_params=_cp(("arbitrary",)), name="gla_decode")(
            qh, kh, z, z, z, lw["gla_w_a2h"], lw["gla_b_ah"], lw["gla_norm_gh"], st)
    return o, st_new.reshape(state.shape)


def _rg_gates(xc, wr_ref, br_ref, wi_ref, bi_ref, lam_ref):
    xb = xc.astype(BF16)
    r = _sigmoid(_dot(xb, wr_ref[...]) + br_ref[...])
    i = _sigmoid(_dot(xb, wi_ref[...]) + bi_ref[...])
    log_a = RG_C * r * _log_sigmoid(lam_ref[...])
    a = jnp.exp(log_a)
    u = jnp.sqrt(1.0 - jnp.exp(2.0 * log_a)) * (i * xc)
    return a, u


def _gelu_tanh(x):
    return 0.5 * x * (1.0 + jnp.tanh(0.7978845608028654 * (x + 0.044715 * x * x * x)))


def _rg_p_body(x_ref, g_ref, cw_ref, cb_ref, wr_ref, br_ref, wi_ref, bi_ref, lam_ref,
               y_ref, h_ref, buf_ref, xe_ref, hc_ref, *, tile):
    t = pl.program_id(1)

    @pl.when(t == 0)
    def _():
        xe_ref[0:8, :] = jnp.zeros((8, RG_W), F32)
        hc_ref[...] = jnp.zeros_like(hc_ref)

    x = x_ref[...]
    xe_ref[8:8 + tile, :] = x
    cw = cw_ref[...]
    xc = cb_ref[...] + x * cw[3:4]
    for j in range(1, CONV_W):
        xc = xc + xe_ref[8 - j:8 - j + tile, :] * cw[3 - j:4 - j]
    a, u = _rg_gates(xc, wr_ref, br_ref, wi_ref, bi_ref, lam_ref)
    row = lax.broadcasted_iota(jnp.int32, (tile, RG_W), 0)
    s = 1
    while s < tile:
        keep = row >= s
        a_prev = jnp.where(keep, pltpu.roll(a, s, 0), 1.0)
        u_prev = jnp.where(keep, pltpu.roll(u, s, 0), 0.0)
        u = a * u_prev + u
        a = a * a_prev
        s *= 2
    h = u + a * hc_ref[...]
    hc_ref[...] = h[tile - 1:tile]
    xe_ref[0:8, :] = x[tile - 8:tile]
    y_ref[...] = h * _gelu_tanh(g_ref[...])

    @pl.when(t == pl.num_programs(1) - 1)
    def _():
        h_ref[0] = h[tile - 1:tile]
        buf_ref[0] = x[tile - (CONV_W - 1):tile]


def rglru_prompt(z, lw, batch, seq_len):
    tile = 256
    nt = seq_len // tile
    rb = lambda b, t: b * nt + t
    full = lambda a: pl.BlockSpec(a.shape, lambda b, t: (0,) * a.ndim)
    ws = (lw["rg_conv_w"], lw["rg_conv_b"], lw["rg_w_r_bd"], lw["rg_b_r"], lw["rg_w_i_bd"], lw["rg_b_i"],
          lw["rg_lambda"])
    return pl.pallas_call(
        functools.partial(_rg_p_body, tile=tile), grid=(batch, nt),
        in_specs=[pl.BlockSpec((tile, RG_W), lambda b, t: (rb(b, t), C_RX // RG_W)),
                  pl.BlockSpec((tile, RG_W), lambda b, t: (rb(b, t), C_RGT // RG_W))] + [full(a) for a in ws],
        out_specs=[pl.BlockSpec((tile, RG_W), lambda b, t: (rb(b, t), 0)),
                   pl.BlockSpec((1, 1, RG_W), lambda b, t: (b, 0, 0)),
                   pl.BlockSpec((1, CONV_W - 1, RG_W), lambda b, t: (b, 0, 0))],
        out_shape=[jax.ShapeDtypeStruct((batch * seq_len, RG_W), F32),
                   jax.ShapeDtypeStruct((batch, 1, RG_W), F32),
                   jax.ShapeDtypeStruct((batch, CONV_W - 1, RG_W), F32)],
        scratch_shapes=[pltpu.VMEM((8 + tile, RG_W), F32), pltpu.VMEM((1, RG_W), F32)],
        compiler_params=_cp(("arbitrary", "arbitrary")), name="rglru_prompt")(z, z, *ws)


def _rg_s_body(x_ref, g_ref, b0_ref, b1_ref, b2_ref, h0_ref, cw_ref, cb_ref, wr_ref, br_ref, wi_ref, bi_ref,
               lam_ref, y_ref, h_ref):
    cw = cw_ref[...]
    x = x_ref[...]
    xc = cb_ref[...] + b0_ref[...] * cw[0:1] + b1_ref[...] * cw[1:2] + b2_ref[...] * cw[2:3] + x * cw[3:4]
    a, u = _rg_gates(xc, wr_ref, br_ref, wi_ref, bi_ref, lam_ref)
    h = u + a * h0_ref[...]
    h_ref[...] = h
    y_ref[...] = h * _gelu_tanh(g_ref[...])


def rglru_decode(z, conv_buf, h0, lw):
    s = z.shape[0]
    buf = conv_buf.reshape(s, (CONV_W - 1) * RG_W)
    full = lambda a: pl.BlockSpec(a.shape, lambda i: (0,) * a.ndim)
    ws = (lw["rg_conv_w"], lw["rg_conv_b"], lw["rg_w_r_bd"], lw["rg_b_r"], lw["rg_w_i_bd"], lw["rg_b_i"],
          lw["rg_lambda"])
    blk = lambda c: pl.BlockSpec((s, RG_W), lambda i: (0, c))
    y, h = pl.pallas_call(
        _rg_s_body, grid=(1,),
        in_specs=[blk(C_RX // RG_W), blk(C_RGT // RG_W), blk(0), blk(1), blk(2), blk(0)] + [full(a) for a in ws],
        out_specs=[blk(0), blk(0)],
        out_shape=[jax.ShapeDtypeStruct((s, RG_W), F32), jax.ShapeDtypeStruct((s, RG_W), F32)],
        compiler_params=_cp(("arbitrary",)), name="rglru_decode")(z, z, buf, buf, buf, h0, *ws)
    buf_new = jnp.concatenate([conv_buf[:, 1:], z[:, None, C_RX:C_RX + RG_W]], axis=1)
    return y, buf_new, h


def _cmp_weights(lg):
    e = jnp.exp(lg - jnp.max(lg, axis=-1, keepdims=True))
    return e / (jnp.sum(e, axis=-1, keepdims=True) * (CMP_BLOCK / lg.shape[-1]))


def _topk_mask(vals, lane, k):
    sel = jnp.zeros(vals.shape, jnp.bool_)
    idxs, oks = [], []
    lane = lane.astype(F32)
    big = float(vals.shape[-1])
    for _ in range(k):
        m = jnp.max(vals, axis=-1, keepdims=True)
        idx = jnp.min(jnp.where(vals == m, lane, big), axis=-1, keepdims=True)
        pick = lane == idx
        ok = m > -jnp.inf
        sel = sel | (pick & ok)
        vals = jnp.where(pick, -jnp.inf, vals)
        idxs.append(idx)
        oks.append(ok)
    return sel, idxs, oks


def _topk_rank_mask(vals, n, k):
    vt = vals.T[:n]
    cidx = lax.broadcasted_iota(jnp.int32, (n, 1), 0)
    rank = jnp.zeros(vt.shape, F32)
    for j in range(n):
        rj = vt[j:j + 1]
        ahead = (rj > vt) | ((rj == vt) & (cidx > j))
        rank = rank + jnp.where(ahead, 1.0, 0.0)
    sel_t = (rank < k) & (vt > -jnp.inf)
    return jnp.where(sel_t, 1.0, 0.0).T > 0.5


def _nsa_p_body(q_ref, g_ref, kvc_ref, kvs_ref, kvw_ref, lg_ref, o_ref,
                kcb_ref, vcb_ref, ks_ref, vs_ref, kw_ref, vw_ref, *, seq_len):
    i = pl.program_id(1)
    qb = QBLOCK
    n_cmp = seq_len // CMP_BLOCK
    tk = 512
    rows = NSA_H * qb

    @pl.when(i == 0)
    def _():
        wc = _cmp_weights(lg_ref[...])
        blk = lax.broadcasted_iota(jnp.int32, (n_cmp, seq_len), 0)
        pos = lax.broadcasted_iota(jnp.int32, (n_cmp, seq_len), 1)
        wmat = jnp.where((pos // CMP_BLOCK) == blk, wc, 0.0).astype(BF16)
        kvcb = _dot(wmat, kvc_ref[...].astype(BF16))
        kcb_ref[...] = kvcb[:, :NSA_D].astype(BF16)
        vcb_ref[...] = kvcb[:, NSA_D:].astype(BF16)
        kvs = kvs_ref[...]
        ks_ref[...] = kvs[:, :NSA_D].astype(BF16)
        vs_ref[...] = kvs[:, NSA_D:].astype(BF16)
        kvw = kvw_ref[...]
        kw_ref[...] = kvw[:, :NSA_D].astype(BF16)
        vw_ref[...] = kvw[:, NSA_D:].astype(BF16)

    q = q_ref[...] * NSA_SCALE
    q2 = jnp.concatenate([q[:, h * NSA_D:(h + 1) * NSA_D] for h in range(NSA_H)], axis=0).astype(BF16)
    t0 = i * qb
    tq = t0 + lax.broadcasted_iota(jnp.int32, (qb, 1), 0)
    slope3 = _alibi_slope(lax.broadcasted_iota(jnp.int32, (NSA_H, 1, 1), 0))

    lane_c = lax.broadcasted_iota(jnp.int32, (1, n_cmp), 1)
    dist_c = (tq - (lane_c * CMP_BLOCK + (CMP_BLOCK - 1))).astype(F32)
    s_c = _dot_nt(q2, kcb_ref[...]).reshape(NSA_H, qb, n_cmp) - slope3 * dist_c[None]
    p_c = _softmax_rows(s_c, (dist_c >= 0)[None])
    o_c = _dot(p_c.reshape(rows, n_cmp).astype(BF16), vcb_ref[...])

    imp = jnp.sum(p_c, axis=0)
    n_sel = seq_len // SEL_BLOCK
    pr = lax.broadcasted_iota(jnp.int32, (n_cmp, n_cmp), 0)
    pc = lax.broadcasted_iota(jnp.int32, (n_cmp, n_cmp), 1)
    pair = ((pr // (SEL_BLOCK // CMP_BLOCK)) == pc).astype(F32)
    imp = _dot(imp, pair, HIGHEST)
    forced = (lane_c == tq // SEL_BLOCK) | (lane_c == 0)
    causal = (lane_c * SEL_BLOCK <= tq) & (lane_c < n_sel)
    imp = jnp.where(forced, jnp.inf, jnp.where(causal, imp, -jnp.inf))
    sel_f = jnp.where(_topk_rank_mask(imp, n_sel, min(SEL_TOPK, n_sel)), 1.0, 0.0)
    sel_b = sel_f.astype(BF16)
    n_tiles = (t0 + qb + tk - 1) // tk
    blk_any = jnp.max(sel_f, axis=0, keepdims=True)
    blk_tile = (lax.broadcasted_iota(jnp.int32, (1, n_sel), 1) // (tk // SEL_BLOCK)).astype(F32)
    far = jnp.float32(seq_len // tk)
    j_lo = jnp.min(jnp.where((blk_any > 0.5) & (blk_tile >= 1.0), blk_tile, far)).astype(jnp.int32)
    j_lo = jnp.minimum(j_lo, n_tiles)

    def sel_tile(jj, carry):
        m, l, acc = carry
        j = jnp.where(jj == 0, 0, j_lo + jj - 1)
        k0 = pl.multiple_of(j * tk, tk)
        kt = ks_ref[pl.ds(k0, tk), :]
        vt = vs_ref[pl.ds(k0, tk), :]
        kpos = k0 + lax.broadcasted_iota(jnp.int32, (1, tk), 1)
        er = lax.broadcasted_iota(jnp.int32, (n_sel, tk), 0)
        ec = k0 + lax.broadcasted_iota(jnp.int32, (n_sel, tk), 1)
        expand = ((ec // SEL_BLOCK) == er).astype(BF16)
        ok = (_dot(sel_b, expand) > 0.5) & (kpos <= tq)
        dist = (tq - kpos).astype(F32)
        bias = jnp.where(ok, 0.0, NEG)
        s = _dot_nt(q2, kt).reshape(NSA_H, qb, tk) - slope3 * dist[None] + bias[None]
        s = s.reshape(rows, tk)
        m_new = jnp.maximum(m, jnp.max(s, axis=-1, keepdims=True))
        alpha = jnp.exp(m - m_new)
        p = jnp.exp(s - m_new)
        l = alpha * l + jnp.sum(p, axis=-1, keepdims=True)
        acc = alpha * acc + _dot(p.astype(BF16), vt)
        return m_new, l, acc

    m0 = jnp.full((rows, 1), NEG, F32)
    _, l_s, acc_s = lax.fori_loop(0, 1 + n_tiles - j_lo, sel_tile,
                                  (m0, jnp.zeros((rows, 1), F32), jnp.zeros((rows, NSA_D), F32)))
    o_s = acc_s / l_s

    wk = min(WINDOW + qb, seq_len)
    w0 = pl.multiple_of(jnp.maximum(t0 - WINDOW, 0), qb)
    wpos = w0 + lax.broadcasted_iota(jnp.int32, (1, wk), 1)
    dist_w = tq - wpos
    ok_w = (dist_w >= 0) & (dist_w <= WINDOW)
    s_w = _dot_nt(q2, kw_ref[pl.ds(w0, wk), :]).reshape(NSA_H, qb, wk) - slope3 * dist_w.astype(F32)[None]
    p_w = _softmax_rows(s_w, ok_w[None])
    o_w = _dot(p_w.reshape(rows, wk).astype(BF16), vw_ref[pl.ds(w0, wk), :])

    gi = _sigmoid(g_ref[...])
    outs = []
    for h in range(NSA_H):
        r = slice(h * qb, (h + 1) * qb)
        outs.append(gi[:, 3 * h:3 * h + 1] * o_c[r] + gi[:, 3 * h + 1:3 * h + 2] * o_s[r]
                    + gi[:, 3 * h + 2:3 * h + 3] * o_w[r])
    o_ref[...] = jnp.concatenate(outs, axis=1)


def nsa_prompt(z, lw, batch, seq_len):
    nq = seq_len // QBLOCK
    n_cmp = seq_len // CMP_BLOCK
    rb = lambda b, i: b * nq + i
    kv = lambda c: pl.BlockSpec((seq_len, LANE), lambda b, i: (b, c))
    lg = jnp.tile(lw["nsa_cmp_logits"], seq_len // CMP_BLOCK)[None]
    return pl.pallas_call(
        functools.partial(_nsa_p_body, seq_len=seq_len), grid=(batch, nq),
        in_specs=[pl.BlockSpec((QBLOCK, 512), lambda b, i: (rb(b, i), C_NQ // 512)),
                  pl.BlockSpec((QBLOCK, LANE), lambda b, i: (rb(b, i), C_NG // LANE)),
                  kv(C_NKV // LANE), kv(C_NKV // LANE + 1), kv(C_NKV // LANE + 2),
                  pl.BlockSpec((1, seq_len), lambda b, i: (0, 0))],
        out_specs=pl.BlockSpec((QBLOCK, 512), lambda b, i: (rb(b, i), 0)),
        out_shape=jax.ShapeDtypeStruct((batch * seq_len, 512), F32),
        scratch_shapes=[pltpu.VMEM((n_cmp, NSA_D), BF16), pltpu.VMEM((n_cmp, NSA_D), BF16)]
        + [pltpu.VMEM((seq_len, NSA_D), BF16)] * 4,
        compiler_params=_cp(("arbitrary", "arbitrary")), name="nsa_prompt")(z, z, z, z, z, lg)


def _mla_p_body(q_ref, k_ref, o_ref):
    i = pl.program_id(1)
    qb, tk = QBLOCK, 512
    rows = MLA_H * qb
    q2 = q_ref[...].reshape(rows, MLA_QW)
    tq = i * qb + (lax.broadcasted_iota(jnp.int32, (rows, 1), 0) & (qb - 1))

    def tile(j, carry, masked):
        m, l, acc = carry
        k0 = pl.multiple_of(j * tk, tk)
        kt = k_ref[pl.ds(k0, tk), :]
        s = _dot_nt(q2, kt)
        if masked:
            kpos = k0 + lax.broadcasted_iota(jnp.int32, (1, tk), 1)
            s = jnp.where(kpos <= tq, s, NEG)
        m_new = jnp.maximum(m, jnp.max(s, axis=-1, keepdims=True))
        alpha = jnp.exp(m - m_new)
        p = jnp.exp(s - m_new)
        l = alpha * l + jnp.sum(p, axis=-1, keepdims=True)
        acc = alpha * acc + _dot(p.astype(BF16), kt[:, :MLA_KVR])
        return m_new, l, acc

    n_full = (i * qb) // tk
    m0 = jnp.full((rows, 1), NEG, F32)
    carry = lax.fori_loop(0, n_full, functools.partial(tile, masked=False),
                          (m0, jnp.zeros((rows, 1), F32), jnp.zeros((rows, MLA_KVR), F32)))
    _, l, acc = tile(n_full, carry, True)
    o = (acc / l).astype(BF16)
    for h in range(MLA_H):
        o_ref[:, h * MLA_KVR:(h + 1) * MLA_KVR] = o[h * qb:(h + 1) * qb]


def mla_prompt(qm, km, batch, seq_len):
    nq = seq_len // QBLOCK
    return pl.pallas_call(
        _mla_p_body, grid=(batch, nq),
        in_specs=[pl.BlockSpec((MLA_H, QBLOCK, MLA_QW), lambda b, i: (0, b * nq + i, 0)),
                  pl.BlockSpec((seq_len, MLA_QW), lambda b, i: (b, 0))],
        out_specs=pl.BlockSpec((QBLOCK, MLA_H * MLA_KVR), lambda b, i: (b * nq + i, 0)),
        out_shape=jax.ShapeDtypeStruct((batch * seq_len, MLA_H * MLA_KVR), BF16),
        compiler_params=_cp(("arbitrary", "arbitrary")), name="mla_prompt")(qm, km)


CH_PAGES = 16
CH_ROWS = CH_PAGES * PAGE
HPAD = 16
MLA_SLOTS = 4
CMP_SLOTS = 8


def _chunk_copies(cache_ref, layer, pt_ref, buf_ref, sem_ref, seq, chunk, slot, feats):
    copies = []
    for p in range(CH_PAGES):
        page = pt_ref[seq, chunk * CH_PAGES + p]
        src = cache_ref.at[layer, page] if feats is None else cache_ref.at[layer, page, pl.ds(0, feats), :]
        copies.append(pltpu.make_async_copy(src, buf_ref.at[slot, :, pl.ds(p * PAGE, PAGE)], sem_ref.at[slot]))
    return copies


def _paged_pipeline(cache_ref, layer, pt_ref, buf_ref, sem_ref, n_chunks, feats, compute):
    s = pl.program_id(0)
    n_seq = pl.num_programs(0)
    n_slots = buf_ref.shape[0]
    ahead = n_slots - 1
    assert n_chunks % n_slots == 0
    args = (cache_ref, layer, pt_ref, buf_ref, sem_ref)

    @pl.when(s == 0)
    def _():
        for c0 in range(ahead):
            for cp in _chunk_copies(*args, 0, c0, c0, feats):
                cp.start()

    def body(c, carry):
        slot = c % n_slots
        tgt = c + ahead
        wrap = tgt >= n_chunks
        s2 = jnp.where(wrap, s + 1, s)
        c2 = jnp.where(wrap, tgt - n_chunks, tgt)

        @pl.when(s2 < n_seq)
        def _():
            for cp in _chunk_copies(*args, s2, c2, c2 % n_slots, feats):
                cp.start()

        for cp in _chunk_copies(*args, s, c, slot, feats):
            cp.wait()
        return compute(c, slot, carry)

    return body


def _nsa_cmp_body(pt_ref, q_ref, lg_ref, cache_ref, oc_ref, imp_ref, buf_ref, sem_ref, wm_ref, kv_ref,
                  *, layer, n_chunks, t_pos):
    s = pl.program_id(0)
    cmp_rows = CH_ROWS // CMP_BLOCK

    @pl.when(s == 0)
    def _():
        wc = _cmp_weights(lg_ref[...])
        blk = lax.broadcasted_iota(jnp.int32, (cmp_rows, CH_ROWS), 0)
        pos = lax.broadcasted_iota(jnp.int32, (cmp_rows, CH_ROWS), 1)
        wm_ref[...] = jnp.where((pos // CMP_BLOCK) == blk, wc, 0.0).astype(BF16)

    def compute(c, slot, carry):
        r0 = pl.multiple_of(c * cmp_rows, cmp_rows)
        kv_ref[pl.ds(r0, cmp_rows), :] = _dot_nt(wm_ref[...], buf_ref[slot].astype(BF16))
        return carry

    body = _paged_pipeline(cache_ref, layer, pt_ref, buf_ref, sem_ref, n_chunks, 2 * NSA_D, compute)
    lax.fori_loop(0, n_chunks, body, 0)

    n_cmp = n_chunks * cmp_rows
    kvcb = kv_ref[...]
    q2 = (q_ref[0] * NSA_SCALE).astype(BF16)
    hrow = lax.broadcasted_iota(jnp.int32, (HPAD, 1), 0)
    slope = _alibi_slope(jnp.minimum(hrow, NSA_H - 1))
    lane_c = lax.broadcasted_iota(jnp.int32, (1, n_cmp), 1)
    dist_c = (t_pos - (lane_c * CMP_BLOCK + (CMP_BLOCK - 1))).astype(F32)
    s_c = _dot_nt(q2, kvcb[:, :NSA_D].astype(BF16)) - slope * dist_c
    p_c = _softmax_rows(s_c, dist_c >= 0)
    oc_ref[0] = _dot(p_c.astype(BF16), kvcb[:, NSA_D:].astype(BF16))
    imp_ref[0] = jnp.sum(jnp.where(hrow < NSA_H, p_c, 0.0), axis=0, keepdims=True)


def nsa_decode_cmp(q16, page_table, cache4, lw, layer):
    n_seq, n_pages = page_table.shape
    n_chunks = n_pages // CH_PAGES
    n_cmp = n_pages * PAGE // CMP_BLOCK
    n_slots = min(CMP_SLOTS, n_chunks)
    lg = jnp.tile(lw["nsa_cmp_logits"], CH_ROWS // CMP_BLOCK)[None]
    gs = pltpu.PrefetchScalarGridSpec(
        num_scalar_prefetch=1, grid=(n_seq,),
        in_specs=[pl.BlockSpec((1, HPAD, NSA_D), lambda s, pt: (s, 0, 0)),
                  pl.BlockSpec((1, CH_ROWS), lambda s, pt: (0, 0)),
                  pl.BlockSpec(memory_space=pl.ANY)],
        out_specs=[pl.BlockSpec((1, HPAD, NSA_D), lambda s, pt: (s, 0, 0)),
                   pl.BlockSpec((1, 1, n_cmp), lambda s, pt: (s, 0, 0))],
        scratch_shapes=[pltpu.VMEM((n_slots, 2 * NSA_D, CH_ROWS), F32), pltpu.SemaphoreType.DMA((n_slots,)),
                        pltpu.VMEM((CH_ROWS // CMP_BLOCK, CH_ROWS), BF16), pltpu.VMEM((n_cmp, LANE), F32)])
    return pl.pallas_call(
        functools.partial(_nsa_cmp_body, layer=layer, n_chunks=n_chunks, t_pos=n_pages * PAGE),
        grid_spec=gs,
        out_shape=[jax.ShapeDtypeStruct((n_seq, HPAD, NSA_D), F32), jax.ShapeDtypeStruct((n_seq, 1, n_cmp), F32)],
        compiler_params=_cp(("arbitrary",)), name="nsa_decode_cmp")(page_table, q16, lg, cache4)


def _nsa_topk_body(imp_ref, idx_ref, *, t_pos):
    imp = imp_ref[...]
    n_seq, n_cmp = imp.shape
    n_sel = -(-(t_pos + 1) // SEL_BLOCK)
    width = idx_ref.shape[-1]
    pr = lax.broadcasted_iota(jnp.int32, (n_cmp, width), 0)
    pc = lax.broadcasted_iota(jnp.int32, (n_cmp, width), 1)
    pair = ((pr // (SEL_BLOCK // CMP_BLOCK)) == pc).astype(F32)
    vals = _dot(imp, pair, HIGHEST)
    lane = lax.broadcasted_iota(jnp.int32, (1, width), 1)
    forced = (lane == t_pos // SEL_BLOCK) | (lane == 0)
    causal = (lane * SEL_BLOCK <= t_pos) & (lane < n_sel)
    vals = jnp.where(forced, jnp.inf, jnp.where(causal, vals, -jnp.inf))
    _, idxs, oks = _topk_mask(vals, lane, min(SEL_TOPK, n_sel))
    out = jnp.full((n_seq, width), -1, jnp.int32)
    for r, (idx, ok) in enumerate(zip(idxs, oks)):
        out = jnp.where(lane == r, jnp.where(ok, idx.astype(jnp.int32), -1), out)
    idx_ref[...] = out


def nsa_decode_topk(imp, t_pos):
    n_seq, n_cmp = imp.shape
    width = 384
    return pl.pallas_call(
        functools.partial(_nsa_topk_body, t_pos=t_pos), grid=(1,),
        in_specs=[pl.BlockSpec((n_seq, n_cmp), lambda i: (0, 0))],
        out_specs=pl.BlockSpec((n_seq, width), lambda i: (0, 0)),
        out_shape=jax.ShapeDtypeStruct((n_seq, width), jnp.int32),
        compiler_params=_cp(("arbitrary",)), name="nsa_decode_topk")(imp)


def _sel_copies(cache_ref, layer, pt_ref, idx_ref, buf_ref, sem_ref, seq, slot, n_blk):
    per_page = PAGE // SEL_BLOCK
    copies = []
    for r in range(SEL_TOPK):
        blk = jnp.clip(idx_ref[seq, r], 0, n_blk - 1)
        page = pt_ref[seq, blk // per_page]
        src = cache_ref.at[layer, page, pl.ds(2 * NSA_D, 2 * NSA_D), :]
        copies.append(pltpu.make_async_copy(src, buf_ref.at[slot, :, pl.ds(r * PAGE, PAGE)], sem_ref.at[slot]))
    return copies


def _attend_with_new(q2, slope, kt_b, vt_b, dist, ok, q_f32, k_new, v_new):
    s = jnp.where(ok, _dot(q2, kt_b) - slope * dist, NEG)
    s_new = jnp.sum(q_f32 * k_new, axis=-1, keepdims=True)
    m = jnp.maximum(jnp.max(s, axis=-1, keepdims=True), s_new)
    e = jnp.exp(s - m)
    e_new = jnp.exp(s_new - m)
    d = jnp.sum(e, axis=-1, keepdims=True) + e_new
    return (_dot_nt(e.astype(BF16), vt_b) + e_new * v_new) / d


def _nsa_sel_body(pt_ref, idx_ref, q_ref, g_ref, oc_ref, new_ref, win_ref, cache_ref, o_ref, buf_ref, sem_ref,
                  *, layer, t_pos):
    s = pl.program_id(0)
    n_seq = pl.num_programs(0)
    n_blk = t_pos // SEL_BLOCK
    slot = s % 2
    args = (cache_ref, layer, pt_ref, idx_ref, buf_ref, sem_ref)

    @pl.when(s == 0)
    def _():
        for cp in _sel_copies(*args, 0, 0, n_blk):
            cp.start()

    @pl.when(s + 1 < n_seq)
    def _():
        for cp in _sel_copies(*args, s + 1, 1 - slot, n_blk):
            cp.start()

    for cp in _sel_copies(*args, s, slot, n_blk):
        cp.wait()

    q = q_ref[0] * NSA_SCALE
    q2 = q.astype(BF16)
    hrow = lax.broadcasted_iota(jnp.int32, (HPAD, 1), 0)
    slope = _alibi_slope(jnp.minimum(hrow, NSA_H - 1))
    new = new_ref[0]
    per_page = PAGE // SEL_BLOCK
    nk = SEL_TOPK * PAGE
    lane = lax.broadcasted_iota(jnp.int32, (1, nk), 1)
    blk_of = jnp.zeros((1, nk), jnp.int32)
    for r in range(SEL_TOPK):
        blk_of = jnp.where(lane // PAGE == r, idx_ref[s, r], blk_of)
    in_page = lane % PAGE
    pos = (blk_of // per_page) * PAGE + in_page
    ok = (blk_of >= 0) & (blk_of < n_blk) & (in_page // SEL_BLOCK == blk_of % per_page)
    kv = buf_ref[slot]
    o_s = _attend_with_new(q2, slope, kv[:NSA_D].astype(BF16), kv[NSA_D:].astype(BF16),
                           (t_pos - pos).astype(F32), ok, q, new[:, 2 * NSA_D:3 * NSA_D], new[:, 3 * NSA_D:4 * NSA_D])
    wkv = win_ref[0]
    wbuf = wkv.shape[1]
    lane_w = lax.broadcasted_iota(jnp.int32, (1, wbuf), 1)
    dist_w = wbuf - lane_w
    ok_w = dist_w <= WINDOW
    o_w = _attend_with_new(q2, slope, wkv[:NSA_D].astype(BF16), wkv[NSA_D:].astype(BF16),
                           dist_w.astype(F32), ok_w, q, new[:, 4 * NSA_D:5 * NSA_D], new[:, 5 * NSA_D:6 * NSA_D])
    gi = _sigmoid(g_ref[0])
    o_ref[0] = gi[:, 0:1] * oc_ref[0] + gi[:, 1:2] * o_s + gi[:, 2:3] * o_w


def nsa_decode_sel(q16, g16, o_c, new_rows, win, page_table, idx, cache4, layer):
    n_seq, n_pages = page_table.shape
    wbuf = win.shape[2]
    gs = pltpu.PrefetchScalarGridSpec(
        num_scalar_prefetch=2, grid=(n_seq,),
        in_specs=[pl.BlockSpec((1, HPAD, NSA_D), lambda s, pt, ix: (s, 0, 0)),
                  pl.BlockSpec((1, HPAD, 3), lambda s, pt, ix: (s, 0, 0)),
                  pl.BlockSpec((1, HPAD, NSA_D), lambda s, pt, ix: (s, 0, 0)),
                  pl.BlockSpec((1, 1, 512), lambda s, pt, ix: (s, 0, 0)),
                  pl.BlockSpec((1, 2 * NSA_D, wbuf), lambda s, pt, ix: (layer * n_seq + s, 0, 0)),
                  pl.BlockSpec(memory_space=pl.ANY)],
        out_specs=pl.BlockSpec((1, HPAD, NSA_D), lambda s, pt, ix: (s, 0, 0)),
        scratch_shapes=[pltpu.VMEM((2, 2 * NSA_D, SEL_TOPK * PAGE), F32), pltpu.SemaphoreType.DMA((2,))])
    return pl.pallas_call(
        functools.partial(_nsa_sel_body, layer=layer, t_pos=n_pages * PAGE),
        grid_spec=gs, out_shape=jax.ShapeDtypeStruct((n_seq, HPAD, NSA_D), F32),
        compiler_params=_cp(("arbitrary",)), name="nsa_decode_sel")(
            page_table, idx, q16, g16, o_c, new_rows, win, cache4)


def _mla_s_body(pt_ref, q_ref, new_ref, cache_ref, o_ref, buf_ref, sem_ref, *, layer, n_chunks):
    q = q_ref[0]
    width = MLA_KVR + MLA_DR
    qk = q[:, :width]

    def compute(c, slot, carry):
        m, l, acc = carry
        kt = buf_ref[slot].astype(BF16)
        s = _dot(qk, kt)
        m_new = jnp.maximum(m, jnp.max(s, axis=-1, keepdims=True))
        alpha = jnp.exp(m - m_new)
        p = jnp.exp(s - m_new)
        l = alpha * l + jnp.sum(p, axis=-1, keepdims=True)
        acc = alpha * acc + _dot_nt(p.astype(BF16), kt[:MLA_KVR])
        return m_new, l, acc

    body = _paged_pipeline(cache_ref, layer, pt_ref, buf_ref, sem_ref, n_chunks, None, compute)
    m, l, acc = lax.fori_loop(0, n_chunks, body, (jnp.full((HPAD, 1), NEG, F32), jnp.zeros((HPAD, 1), F32),
                                                  jnp.zeros((HPAD, MLA_KVR), F32)))
    k_new = new_ref[0].astype(F32)
    s_new = jnp.sum(q.astype(F32) * k_new, axis=-1, keepdims=True)
    m_new = jnp.maximum(m, s_new)
    alpha = jnp.exp(m - m_new)
    e_new = jnp.exp(s_new - m_new)
    o_ref[0] = (alpha * acc + e_new * k_new[:, :MLA_KVR]) / (alpha * l + e_new)


def mla_decode(q16, k_new, page_table, cache_mla, layer):
    n_seq, n_pages = page_table.shape
    width = MLA_KVR + MLA_DR
    n_slots = min(MLA_SLOTS, n_pages // CH_PAGES)
    gs = pltpu.PrefetchScalarGridSpec(
        num_scalar_prefetch=1, grid=(n_seq,),
        in_specs=[pl.BlockSpec((1, HPAD, MLA_QW), lambda s, pt: (s, 0, 0)),
                  pl.BlockSpec((1, 1, MLA_QW), lambda s, pt: (s, 0, 0)),
                  pl.BlockSpec(memory_space=pl.ANY)],
        out_specs=pl.BlockSpec((1, HPAD, MLA_KVR), lambda s, pt: (s, 0, 0)),
        scratch_shapes=[pltpu.VMEM((n_slots, width, CH_ROWS), F32), pltpu.SemaphoreType.DMA((n_slots,))])
    return pl.pallas_call(
        functools.partial(_mla_s_body, layer=layer, n_chunks=n_pages // CH_PAGES),
        grid_spec=gs, out_shape=jax.ShapeDtypeStruct((n_seq, HPAD, MLA_KVR), F32),
        compiler_params=_cp(("arbitrary",)), name="mla_decode")(page_table, q16, k_new, cache_mla)


def _merge_body(gla_ref, nsa_ref, rg_ref, lat_ref, mg_ref, x_ref, wuv_ref, wb_ref, wo_ref, g_ref, b_ref, o_ref):
    o_mla = _dot(lat_ref[...], wuv_ref[...])
    branches = (gla_ref[...], nsa_ref[...], rg_ref[...], o_mla)
    merged = None
    for n, br in enumerate(branches):
        proj = _dot(br.astype(BF16), wb_ref[n])
        term = _sigmoid(mg_ref[:, n * D_MODEL:(n + 1) * D_MODEL]) * proj
        merged = term if merged is None else merged + term
    y = DN_ALPHA * x_ref[...] + _dot(merged.astype(BF16), wo_ref[...])
    o_ref[...] = _layer_norm(y, g_ref[...], b_ref[...])


def merge(o_gla, o_nsa, o_rg, o_lat, z, x, lw):
    n = x.shape[0]
    tm = min(n, 256)
    row = lambda w: pl.BlockSpec((tm, w), lambda i: (i, 0))
    full = lambda a: pl.BlockSpec(a.shape, lambda i: (0,) * a.ndim)
    ws = (lw["w_uv_bd"], lw["w_branch"], lw["w_out"], lw["ln1_g"], lw["ln1_b"])
    return pl.pallas_call(
        _merge_body, grid=(n // tm,),
        in_specs=[row(512), row(512), row(512), row(MLA_H * MLA_KVR), row(N_BRANCH * D_MODEL), row(D_MODEL)]
        + [full(a) for a in ws],
        out_specs=row(D_MODEL), out_shape=jax.ShapeDtypeStruct((n, D_MODEL), F32),
        compiler_params=_cp(("parallel",)), name="merge")(o_gla, o_nsa, o_rg, o_lat, z, x, *ws)


def _ffn_body(x_ref, wu_ref, wd_ref, g_ref, b_ref, o_ref, xb_ref, acc_ref):
    f = pl.program_id(1)

    @pl.when(f == 0)
    def _():
        xb_ref[...] = x_ref[...].astype(BF16)
        acc_ref[...] = jnp.zeros_like(acc_ref)

    h = jnp.maximum(_dot(xb_ref[...], wu_ref[...]), 0.0)
    acc_ref[...] += _dot((h * h).astype(BF16), wd_ref[...])

    @pl.when(f == pl.num_programs(1) - 1)
    def _():
        o_ref[...] = _layer_norm(DN_ALPHA * x_ref[...] + acc_ref[...], g_ref[...], b_ref[...])


def ffn(x, lw):
    n = x.shape[0]
    tm, tf = min(n, 1024), 1024
    vec = pl.BlockSpec((1, D_MODEL), lambda i, f: (0, 0))
    return pl.pallas_call(
        _ffn_body, grid=(n // tm, D_FF // tf),
        in_specs=[pl.BlockSpec((tm, D_MODEL), lambda i, f: (i, 0)),
                  pl.BlockSpec((D_MODEL, tf), lambda i, f: (0, f)),
                  pl.BlockSpec((tf, D_MODEL), lambda i, f: (f, 0)), vec, vec],
        out_specs=pl.BlockSpec((tm, D_MODEL), lambda i, f: (i, 0)),
        out_shape=jax.ShapeDtypeStruct((n, D_MODEL), F32),
        scratch_shapes=[pltpu.VMEM((tm, D_MODEL), BF16), pltpu.VMEM((tm, D_MODEL), F32)],
        compiler_params=_cp(("parallel", "arbitrary")), name="ffn")(x, lw["w_up"], lw["w_down"], lw["ln2_g"],
                                                                      lw["ln2_b"])


def _pad_cols(a, width):
    return jnp.pad(a, ((0, 0), (0, width - a.shape[1])))


def _block_diag(blocks):
    n, r, c = blocks.shape
    eye = jnp.eye(n, dtype=blocks.dtype)
    return (eye[:, None, :, None] * blocks[:, :, None, :]).reshape(n * r, n * c)


def _layer_weights(l, w_in, gla_w_a2, gla_b_a, gla_norm_g, nsa_cmp_logits, rg_conv_w, rg_conv_b, rg_w_r, rg_b_r,
                   rg_w_i, rg_b_i, rg_lambda, mla_q_g, mla_w_uq, mla_kv_g, mla_w_uk, mla_w_uv, w_branch, w_out,
                   ln1_g, ln1_b, w_up, w_down, ln2_g, ln2_b):
    (gq, gk, gv, glr, gr, nq, nkv, ng, rx, rgt, mq, mkv, mkr, mg) = jnp.split(w_in[l], IN_SPLITS, axis=1)
    w_in_p = jnp.concatenate(
        [mg, gv, gr, nq, rx, rgt, _pad_cols(nkv, 512), _pad_cols(mq, 512), gq, gk, mkv,
         _pad_cols(glr, LANE), _pad_cols(ng, LANE), _pad_cols(mkr, LANE)], axis=1)
    w_in_p = _pad_cols(w_in_p, DZ).astype(BF16)
    uq = mla_w_uq[l].reshape(MLA_QR, MLA_H, MLA_DN + MLA_DR)
    uq_nope = uq[:, :, :MLA_DN].reshape(MLA_QR, MLA_H * MLA_DN)
    uq_rope = jnp.pad(uq[:, :, MLA_DN:], ((0, 0), (0, 0), (0, LANE - MLA_DR))).reshape(MLA_QR, MLA_H * LANE)
    pad_rows = lambda a: jnp.pad(a, ((0, 512 - MLA_QR), (0, 0)))
    a2p = jnp.pad(gla_w_a2[l], ((0, LANE - GLA_RANK), (0, 0)))
    return {
        "w_in_p": w_in_p,
        "gla_w_a2p": a2p,
        "gla_b_a": gla_b_a[l][None],
        "gla_norm_g": gla_norm_g[l].reshape(1, GLA_H * GLA_DV),
        "gla_w_a2h": a2p.reshape(LANE, GLA_H, GLA_DK).transpose(1, 0, 2),
        "gla_b_ah": gla_b_a[l].reshape(GLA_H, 1, GLA_DK),
        "gla_norm_gh": gla_norm_g[l].reshape(GLA_H, 1, GLA_DV),
        "nsa_cmp_logits": nsa_cmp_logits[l],
        "rg_conv_w": rg_conv_w[l], "rg_conv_b": rg_conv_b[l][None],
        "rg_w_r_bd": _block_diag(rg_w_r[l]).astype(BF16), "rg_b_r": rg_b_r[l][None],
        "rg_w_i_bd": _block_diag(rg_w_i[l]).astype(BF16), "rg_b_i": rg_b_i[l][None],
        "rg_lambda": rg_lambda[l][None],
        "mla_q_g": _pad_cols(mla_q_g[l][None], 512),
        "w_uq_nope": pad_rows(uq_nope).astype(BF16),
        "w_uq_rope": pad_rows(uq_rope).astype(BF16),
        "w_uk_bd": _block_diag(mla_w_uk[l].transpose(1, 2, 0)).astype(BF16),
        "mla_kv_g": mla_kv_g[l][None],
        "w_uv_bd": _block_diag(mla_w_uv[l].transpose(1, 0, 2)).astype(BF16),
        "w_branch": w_branch[l].astype(BF16), "w_out": w_out[l].astype(BF16),
        "ln1_g": ln1_g[l][None], "ln1_b": ln1_b[l][None],
        "w_up": w_up[l].astype(BF16), "w_down": w_down[l].astype(BF16),
        "ln2_g": ln2_g[l][None], "ln2_b": ln2_b[l][None],
    }


def _rope_tables(pos):
    half = MLA_DR // 2
    freq = ROPE_BASE ** (-jnp.arange(half, dtype=F32) / half)
    ang = pos.astype(F32)[:, None] * freq
    cos, sin = jnp.cos(ang), jnp.sin(ang)
    zero = jnp.zeros_like(cos)
    pad = lambda a, b: jnp.pad(jnp.concatenate([a, b], axis=1), ((0, 0), (0, LANE - MLA_DR)))
    return pad(cos, cos), pad(-sin, zero), pad(zero, sin)


def _prompt_layer(x, lw, tabs, batch, seq_len):
    z = in_proj(x, lw["w_in_p"])
    qm, mla_rows, km = attn_prep(z, tabs, lw, seq_len)
    o_gla, s_gla = gla_prompt(z, lw, batch, seq_len)
    o_rg, rg_h, rg_buf = rglru_prompt(z, lw, batch, seq_len)
    o_nsa = nsa_prompt(z, lw, batch, seq_len)
    o_lat = mla_prompt(qm, km, batch, seq_len)
    x = ffn(merge(o_gla, o_nsa, o_rg, o_lat, z, x, lw), lw)
    nkv = z[:, C_NKV:C_NKV + 6 * NSA_D].reshape(batch, seq_len, 6, NSA_D)
    keep = min(WINDOW, seq_len)
    states = (nkv[:, :, :4], nkv[:, seq_len - keep:, 4:], mla_rows.reshape(batch, seq_len, -1), s_gla,
              rg_h.reshape(batch, RG_W), rg_buf)
    return x, states


def _sample_layer(x, lw, tabs, layer, cache4, win3, win_l, cache_mla, page_table, s_gla, rg_buf, rg_h):
    n_seq = x.shape[0]
    t_pos = page_table.shape[1] * PAGE
    z = in_proj(x, lw["w_in_p"])
    qm, mla_rows, km = attn_prep(z, tabs, lw, 1)
    o_gla, s_gla_new = gla_decode(z, s_gla, lw)
    o_rg, rg_buf_new, rg_h_new = rglru_decode(z, rg_buf, rg_h, lw)
    pad_h = lambda a: jnp.pad(a, ((0, 0), (0, HPAD - a.shape[1]), (0, 0)))
    q16 = pad_h(z[:, C_NQ:C_NQ + NSA_H * NSA_D].reshape(n_seq, NSA_H, NSA_D))
    g16 = pad_h(z[:, C_NG:C_NG + 3 * NSA_H].reshape(n_seq, NSA_H, 3))
    o_c, imp = nsa_decode_cmp(q16, page_table, cache4, lw, layer)
    idx = nsa_decode_topk(imp.reshape(n_seq, -1), t_pos)[:, :SEL_TOPK]
    new_rows = z[:, None, C_NKV:C_NKV + 512]
    o_nsa = nsa_decode_sel(q16, g16, o_c, new_rows, win3, page_table, idx, cache4, layer)
    o_nsa = o_nsa[:, :NSA_H].reshape(n_seq, NSA_H * NSA_D)
    qm16 = pad_h(qm.transpose(1, 0, 2))
    o_lat = mla_decode(qm16, km[:, None], page_table, cache_mla, layer)
    o_lat = o_lat[:, :MLA_H].reshape(n_seq, MLA_H * MLA_KVR).astype(BF16)
    x = ffn(merge(o_gla, o_nsa, o_rg, o_lat, z, x, lw), lw)
    nkv = z[:, C_NKV:C_NKV + 6 * NSA_D].reshape(n_seq, 1, 6, NSA_D)
    wkv = jnp.concatenate([win_l, nkv[:, :, 4:]], axis=1)
    keep = min(WINDOW, wkv.shape[1])
    states = (nkv[:, :, :4], wkv[:, wkv.shape[1] - keep:], mla_rows.reshape(n_seq, 1, -1), s_gla_new, rg_h_new,
              rg_buf_new)
    return x, states


def kernel(x_prompt, x_sample, cache_nsa, cache_nsa_win, cache_mla, state_gla, state_rg_h, state_rg_conv,
           page_table, w_in, gla_w_a2, gla_b_a, gla_norm_g, nsa_cmp_logits, rg_conv_w, rg_conv_b, rg_w_r,
           rg_b_r, rg_w_i, rg_b_i, rg_lambda, mla_q_g, mla_w_uq, mla_kv_g, mla_w_uk, mla_w_uv, w_branch,
           w_out, ln1_g, ln1_b, w_up, w_down, ln2_g, ln2_b):
    batch, seq_len, _ = x_prompt.shape
    n_seq, dec_seq, _ = x_sample.shape
    assert dec_seq == 1 and seq_len % 512 == 0 and page_table.shape[1] % CH_PAGES == 0
    past = page_table.shape[1] * PAGE
    depth = w_in.shape[0]
    params = (w_in, gla_w_a2, gla_b_a, gla_norm_g, nsa_cmp_logits, rg_conv_w, rg_conv_b, rg_w_r, rg_b_r, rg_w_i,
              rg_b_i, rg_lambda, mla_q_g, mla_w_uq, mla_kv_g, mla_w_uk, mla_w_uv, w_branch, w_out, ln1_g, ln1_b,
              w_up, w_down, ln2_g, ln2_b)
    tabs_p = _rope_tables(jnp.arange(seq_len, dtype=jnp.int32))
    tabs_s = _rope_tables(jnp.full((n_seq,), past, jnp.int32))
    n_phys = cache_nsa.shape[1]
    cache4 = cache_nsa.transpose(0, 1, 3, 4, 2).reshape(depth, n_phys, 4 * NSA_D, PAGE)
    cache_mla = cache_mla.transpose(0, 1, 3, 2)
    wbuf = cache_nsa_win.shape[2]
    win3 = cache_nsa_win.transpose(0, 1, 3, 4, 2).reshape(depth * n_seq, 2 * NSA_D, wbuf)
    yp = x_prompt.reshape(batch * seq_len, D_MODEL)
    ys = x_sample.reshape(n_seq, D_MODEL)
    st_p, st_s = [], []
    for l in range(depth):
        lw = _layer_weights(l, *params)
        yp, sp = _prompt_layer(yp, lw, tabs_p, batch, seq_len)
        ys, ss = _sample_layer(ys, lw, tabs_s, l, cache4, win3, cache_nsa_win[l], cache_mla, page_table,
                               state_gla[l], state_rg_conv[l], state_rg_h[l])
        st_p.append(sp)
        st_s.append(ss)
    stack = lambda sts, i: jnp.stack([s[i] for s in sts])
    return (yp.reshape(batch, seq_len, D_MODEL), ys.reshape(n_seq, 1, D_MODEL),
            stack(st_p, 0), stack(st_s, 0), stack(st_p, 1), stack(st_s, 1), stack(st_p, 2), stack(st_s, 2),
            stack(st_p, 3), stack(st_s, 3), stack(st_p, 4), stack(st_s, 4), stack(st_p, 5), stack(st_s, 5))
```

```python
import functools

import numpy as np
import jax
import jax.numpy as jnp
from jax import lax
from jax.experimental import pallas as pl
from jax.experimental.pallas import tpu as pltpu

F32 = jnp.float32
BF16 = jnp.bfloat16
HIGHEST = lax.Precision.HIGHEST

D_MODEL = 1024
DEPTH = 2
PAGE = 128
GLA_H, GLA_DK, GLA_DV, GLA_RANK, GLA_TAU, GLA_CHUNK = 4, 64, 128, 16, 16.0, 64
NSA_H, NSA_D = 8, 64
NSA_SCALE = NSA_D ** -0.5
CMP_BLOCK, SEL_BLOCK, SEL_TOPK, WINDOW, QBLOCK = 32, 64, 16, 512, 128
RG_W, RG_BLOCKS, CONV_W, RG_C = 512, 8, 4, 8.0
RG_BD = RG_W // RG_BLOCKS
MLA_H, MLA_QR, MLA_KVR, MLA_DN, MLA_DR, MLA_DV = 8, 384, 256, 64, 32, 64
MLA_SCALE = (MLA_DN + MLA_DR) ** -0.5
ROPE_BASE = 10000.0
N_BRANCH, BRANCH_W = 4, 512
D_FF = 4 * D_MODEL
DN_ALPHA = (2 * DEPTH) ** 0.25
LN_EPS = 1e-5
IN_SIZES = (GLA_H * GLA_DK, GLA_H * GLA_DK, GLA_H * GLA_DV, GLA_RANK, GLA_H * GLA_DV,
            NSA_H * NSA_D, 6 * NSA_D, 3 * NSA_H, RG_W, RG_W, MLA_QR, MLA_KVR, MLA_DR, N_BRANCH * D_MODEL)
IN_SPLITS = tuple(int(v) for v in np.cumsum(IN_SIZES)[:-1])

LANE = 128
VMEM_LIMIT = 56 * 2 ** 20
NEG = -1e30
MLA_QW = MLA_KVR + LANE

C_MG, C_GV, C_GR, C_NQ, C_RX, C_RGT, C_NKV, C_MQ = 0, 4096, 4608, 5120, 5632, 6144, 6656, 7168
C_GQ, C_GK, C_MKV, C_GLR, C_NG, C_MKR = 7680, 7936, 8192, 8448, 8576, 8704
DZ = 9216


def _cp(sem, vmem=VMEM_LIMIT):
    return pltpu.CompilerParams(dimension_semantics=sem, vmem_limit_bytes=vmem)


def _dot(a, b, precision=None):
    return jnp.dot(a, b, preferred_element_type=F32, precision=precision)


def _dot_nt(a, b):
    return lax.dot_general(a, b, (((1,), (1,)), ((), ())), preferred_element_type=F32)


def _dot_tn(a, b, precision=None):
    return lax.dot_general(a, b, (((0,), (0,)), ((), ())), preferred_element_type=F32, precision=precision)


def _log_sigmoid(x):
    return jnp.minimum(x, 0.0) - jnp.log(1.0 + jnp.exp(-jnp.abs(x)))


def _sigmoid(x):
    return 1.0 / (1.0 + jnp.exp(-x))


def _layer_norm(x, g, b):
    mu = jnp.mean(x, axis=-1, keepdims=True)
    xc = x - mu
    var = jnp.mean(xc * xc, axis=-1, keepdims=True)
    return xc * lax.rsqrt(var + LN_EPS) * g + b


def _alibi_slope(h):
    return lax.bitcast_convert_type((126 - h) << 23, F32)


def _softmax_rows(s, valid):
    s = jnp.where(valid, s, -jnp.inf)
    m = jnp.max(s, axis=-1, keepdims=True)
    m = jnp.where(m > -jnp.inf, m, 0.0)
    e = jnp.exp(s - m)
    d = jnp.sum(e, axis=-1, keepdims=True)
    return e / jnp.where(d > 0, d, 1.0)


def _inproj_body(x_ref, w_ref, o_ref, xb_ref):
    @pl.when(pl.program_id(1) == 0)
    def _():
        xb_ref[...] = x_ref[...].astype(BF16)

    o_ref[...] = _dot(xb_ref[...], w_ref[...])


def in_proj(x, w):
    n = x.shape[0]
    tm, tn = min(n, 2048), 1024
    return pl.pallas_call(
        _inproj_body, grid=(n // tm, DZ // tn),
        in_specs=[pl.BlockSpec((tm, D_MODEL), lambda i, j: (i, 0)),
                  pl.BlockSpec((D_MODEL, tn), lambda i, j: (0, j))],
        out_specs=pl.BlockSpec((tm, tn), lambda i, j: (i, j)),
        out_shape=jax.ShapeDtypeStruct((n, DZ), F32),
        scratch_shapes=[pltpu.VMEM((tm, D_MODEL), BF16)],
        compiler_params=_cp(("parallel", "arbitrary")), name="in_proj")(x, w)


def _rope_lanes(x, cos, sa, sb):
    w = x.shape[-1]
    return x * cos + pltpu.roll(x, w - 16, 1) * sa + pltpu.roll(x, 16, 1) * sb


def _prep_body(mq_ref, mkv_ref, mkr_ref, cos_ref, sa_ref, sb_ref, qg_ref, wn_ref, wr_ref, wuk_ref, kvg_ref,
               qm_ref, rows_ref, km_ref):
    mq = mq_ref[...]
    qn = mq * lax.rsqrt(jnp.sum(mq * mq, axis=-1, keepdims=True) * (1.0 / MLA_QR) + 1e-6) * qg_ref[...]
    qb = qn.astype(BF16)
    nope = _dot(qb, wn_ref[...])
    rq = _dot(qb, wr_ref[...])
    qlat = _dot(nope.astype(BF16), wuk_ref[...])
    cos, sa, sb = cos_ref[...], sa_ref[...], sb_ref[...]
    rq = _rope_lanes(rq, jnp.concatenate([cos] * MLA_H, axis=1), jnp.concatenate([sa] * MLA_H, axis=1),
                     jnp.concatenate([sb] * MLA_H, axis=1))
    for h in range(MLA_H):
        qh = jnp.concatenate([qlat[:, h * MLA_KVR:(h + 1) * MLA_KVR], rq[:, h * LANE:(h + 1) * LANE]], axis=1)
        qm_ref[h] = (qh * MLA_SCALE).astype(BF16)
    mkv = mkv_ref[...]
    ckv = mkv * lax.rsqrt(jnp.mean(mkv * mkv, axis=-1, keepdims=True) + 1e-6) * kvg_ref[...]
    kr = _rope_lanes(mkr_ref[...], cos, sa, sb)
    rows_ref[...] = jnp.concatenate([ckv, kr[:, :MLA_DR]], axis=1)
    km_ref[...] = jnp.concatenate([ckv, kr], axis=1).astype(BF16)


def attn_prep(z, tabs, lw, seq_len):
    n = z.shape[0]
    tm = min(n, 512)
    nt = seq_len // tm if seq_len >= tm else 1
    tab_spec = pl.BlockSpec((tm, LANE), lambda i: (i % nt, 0))
    full = lambda a: pl.BlockSpec(a.shape, lambda i: (0,) * a.ndim)
    ws = (lw["mla_q_g"], lw["w_uq_nope"], lw["w_uq_rope"], lw["w_uk_bd"], lw["mla_kv_g"])
    return pl.pallas_call(
        _prep_body, grid=(n // tm,),
        in_specs=[pl.BlockSpec((tm, 512), lambda i: (i, C_MQ // 512)),
                  pl.BlockSpec((tm, 256), lambda i: (i, C_MKV // 256)),
                  pl.BlockSpec((tm, LANE), lambda i: (i, C_MKR // LANE)),
                  tab_spec, tab_spec, tab_spec] + [full(a) for a in ws],
        out_specs=[pl.BlockSpec((MLA_H, tm, MLA_QW), lambda i: (0, i, 0)),
                   pl.BlockSpec((tm, MLA_KVR + MLA_DR), lambda i: (i, 0)),
                   pl.BlockSpec((tm, MLA_QW), lambda i: (i, 0))],
        out_shape=[jax.ShapeDtypeStruct((MLA_H, n, MLA_QW), BF16),
                   jax.ShapeDtypeStruct((n, MLA_KVR + MLA_DR), F32),
                   jax.ShapeDtypeStruct((n, MLA_QW), BF16)],
        compiler_params=_cp(("parallel",)), name="attn_prep")(z, z, z, *tabs, *ws)


def _gla_out(o, gr, ng):
    outs = []
    for h in range(GLA_H):
        oh = o[:, h * GLA_DV:(h + 1) * GLA_DV]
        outs.append(oh * lax.rsqrt(jnp.mean(oh * oh, axis=-1, keepdims=True) + 1e-6))
    y = jnp.concatenate(outs, axis=1) * ng
    return y * (gr * _sigmoid(gr))


def _gla_p_body(q_ref, k_ref, v_ref, gr_ref, glr_ref, wa_ref, ba_ref, ng_ref, o_ref, st_ref, s_ref, *, tile):
    t = pl.program_id(1)
    c = GLA_CHUNK

    @pl.when(t == 0)
    def _():
        s_ref[...] = jnp.zeros_like(s_ref)

    la = _log_sigmoid(_dot(glr_ref[...], wa_ref[...], HIGHEST) + ba_ref[...]) * (1.0 / GLA_TAU)
    row = lax.broadcasted_iota(jnp.int32, (c, c), 0)
    col = lax.broadcasted_iota(jnp.int32, (c, c), 1)
    tri = (row >= col).astype(F32)
    ones = jnp.ones((c, GLA_DV), F32)
    for ci in range(tile // c):
        rows = slice(ci * c, (ci + 1) * c)
        g = la[rows]
        b = _dot(tri, g, HIGHEST)
        blast_rep = _dot_tn(g, ones, HIGHEST)
        q = q_ref[rows, :] * (GLA_DK ** -0.5)
        k = k_ref[rows, :]
        v = v_ref[rows, :]
        outs = []
        for h in range(GLA_H):
            dk = slice(h * GLA_DK, (h + 1) * GLA_DK)
            bh, qh, kh = b[:, dk], q[:, dk], k[:, dk]
            vh = v[:, h * GLA_DV:(h + 1) * GLA_DV].astype(BF16)
            bmid = bh[c // 2 - 1:c // 2]
            blast = bh[c - 1:c]
            s_old = s_ref[h]
            o_inter = _dot((qh * jnp.exp(bh)).astype(BF16), s_old.astype(BF16))
            att = _dot_nt((qh * jnp.exp(bh - bmid)).astype(BF16), (kh * jnp.exp(bmid - bh)).astype(BF16)) * tri
            o_intra = _dot(att.astype(BF16), vh)
            kl = (kh * jnp.exp(blast - bh)).astype(BF16)
            s_ref[h] = jnp.exp(blast_rep[dk]) * s_old + _dot_tn(kl, vh)
            outs.append(o_inter + o_intra)
        o_ref[rows, :] = _gla_out(jnp.concatenate(outs, axis=1), gr_ref[rows, :], ng_ref[...])

    @pl.when(t == pl.num_programs(1) - 1)
    def _():
        st_ref[0] = s_ref[...]


def gla_prompt(z, lw, batch, seq_len):
    tile = 256
    nt = seq_len // tile
    rb = lambda b, t: b * nt + t
    full = lambda a: pl.BlockSpec(a.shape, lambda b, t: (0,) * a.ndim)
    ws = (lw["gla_w_a2p"], lw["gla_b_a"], lw["gla_norm_g"])
    return pl.pallas_call(
        functools.partial(_gla_p_body, tile=tile), grid=(batch, nt),
        in_specs=[pl.BlockSpec((tile, 256), lambda b, t: (rb(b, t), C_GQ // 256)),
                  pl.BlockSpec((tile, 256), lambda b, t: (rb(b, t), C_GK // 256)),
                  pl.BlockSpec((tile, 512), lambda b, t: (rb(b, t), C_GV // 512)),
                  pl.BlockSpec((tile, 512), lambda b, t: (rb(b, t), C_GR // 512)),
                  pl.BlockSpec((tile, LANE), lambda b, t: (rb(b, t), C_GLR // LANE))] + [full(a) for a in ws],
        out_specs=[pl.BlockSpec((tile, 512), lambda b, t: (rb(b, t), 0)),
                   pl.BlockSpec((1, GLA_H, GLA_DK, GLA_DV), lambda b, t: (b, 0, 0, 0))],
        out_shape=[jax.ShapeDtypeStruct((batch * seq_len, 512), F32),
                   jax.ShapeDtypeStruct((batch, GLA_H, GLA_DK, GLA_DV), F32)],
        scratch_shapes=[pltpu.VMEM((GLA_H, GLA_DK, GLA_DV), F32)],
        compiler_params=_cp(("arbitrary", "arbitrary")), name="gla_prompt")(z, z, z, z, z, *ws)


def _gla_s_body(q_ref, k_ref, v_ref, gr_ref, glr_ref, wa_ref, ba_ref, ng_ref, st_ref, o_ref, sto_ref):
    la = _log_sigmoid(_dot(glr_ref[...], wa_ref[0], HIGHEST) + ba_ref[0]) * (1.0 / GLA_TAU)
    ea = jnp.exp(la)
    q = q_ref[0] * (GLA_DK ** -0.5)
    k = k_ref[0]
    v = v_ref[...]
    o = jnp.zeros(v.shape, F32)
    for d in range(GLA_DK):
        s_new = ea[:, d:d + 1] * st_ref[:, d, :] + k[:, d:d + 1] * v
        sto_ref[:, d, :] = s_new
        o = o + q[:, d:d + 1] * s_new
    y = o * lax.rsqrt(jnp.mean(o * o, axis=-1, keepdims=True) + 1e-6) * ng_ref[0]
    gr = gr_ref[...]
    o_ref[...] = y * (gr * _sigmoid(gr))


def gla_decode(z, state, lw):
    s = z.shape[0]
    heads = lambda a: a.reshape(s, GLA_H, GLA_DK).transpose(1, 0, 2)
    qh = heads(z[:, C_GQ:C_GQ + 256])
    kh = heads(z[:, C_GK:C_GK + 256])
    st = state.reshape(s, GLA_H * GLA_DK, GLA_DV)
    o, st_new = pl.pallas_call(
        _gla_s_body, grid=(GLA_H,),
        in_specs=[pl.BlockSpec((1, s, GLA_DK), lambda h: (h, 0, 0)),
                  pl.BlockSpec((1, s, GLA_DK), lambda h: (h, 0, 0)),
                  pl.BlockSpec((s, GLA_DV), lambda h: (0, C_GV // GLA_DV + h)),
                  pl.BlockSpec((s, GLA_DV), lambda h: (0, C_GR // GLA_DV + h)),
                  pl.BlockSpec((s, LANE), lambda h: (0, C_GLR // LANE)),
                  pl.BlockSpec((1, LANE, GLA_DK), lambda h: (h, 0, 0)),
                  pl.BlockSpec((1, 1, GLA_DK), lambda h: (h, 0, 0)),
                  pl.BlockSpec((1, 1, GLA_DV), lambda h: (h, 0, 0)),
                  pl.BlockSpec((s, GLA_DK, GLA_DV), lambda h: (0, h, 0))],
        out_specs=[pl.BlockSpec((s, GLA_DV), lambda h: (0, h)),
                   pl.BlockSpec((s, GLA_DK, GLA_DV), lambda h: (0, h, 0))],
        out_shape=[jax.ShapeDtypeStruct((s, GLA_H * GLA_DV), F32),
                   jax.ShapeDtypeStruct(st.shape, F32)],
        compiler_params=_cp(("arbitrary",)), name="gla_decode")(
            qh, kh, z, z, z, lw["gla_w_a2h"], lw["gla_b_ah"], lw["gla_norm_gh"], st)
    return o, st_new.reshape(state.shape)


def _rg_gates(xc, wr_ref, br_ref, wi_ref, bi_ref, lam_ref):
    xb = xc.astype(BF16)
    r = _sigmoid(_dot(xb, wr_ref[...]) + br_ref[...])
    i = _sigmoid(_dot(xb, wi_ref[...]) + bi_ref[...])
    log_a = RG_C * r * _log_sigmoid(lam_ref[...])
    a = jnp.exp(log_a)
    u = jnp.sqrt(1.0 - jnp.exp(2.0 * log_a)) * (i * xc)
    return a, u


def _gelu_tanh(x):
    return 0.5 * x * (1.0 + jnp.tanh(0.7978845608028654 * (x + 0.044715 * x * x * x)))


def _rg_p_body(x_ref, g_ref, cw_ref, cb_ref, wr_ref, br_ref, wi_ref, bi_ref, lam_ref,
               y_ref, h_ref, buf_ref, xe_ref, hc_ref, *, tile):
    t = pl.program_id(1)

    @pl.when(t == 0)
    def _():
        xe_ref[0:8, :] = jnp.zeros((8, RG_W), F32)
        hc_ref[...] = jnp.zeros_like(hc_ref)

    x = x_ref[...]
    xe_ref[8:8 + tile, :] = x
    cw = cw_ref[...]
    xc = cb_ref[...] + x * cw[3:4]
    for j in range(1, CONV_W):
        xc = xc + xe_ref[8 - j:8 - j + tile, :] * cw[3 - j:4 - j]
    a, u = _rg_gates(xc, wr_ref, br_ref, wi_ref, bi_ref, lam_ref)
    row = lax.broadcasted_iota(jnp.int32, (tile, RG_W), 0)
    s = 1
    while s < tile:
        keep = row >= s
        a_prev = jnp.where(keep, pltpu.roll(a, s, 0), 1.0)
        u_prev = jnp.where(keep, pltpu.roll(u, s, 0), 0.0)
        u = a * u_prev + u
        a = a * a_prev
        s *= 2
    h = u + a * hc_ref[...]
    hc_ref[...] = h[tile - 1:tile]
    xe_ref[0:8, :] = x[tile - 8:tile]
    y_ref[...] = h * _gelu_tanh(g_ref[...])

    @pl.when(t == pl.num_programs(1) - 1)
    def _():
        h_ref[0] = h[tile - 1:tile]
        buf_ref[0] = x[tile - (CONV_W - 1):tile]


def rglru_prompt(z, lw, batch, seq_len):
    tile = 256
    nt = seq_len // tile
    rb = lambda b, t: b * nt + t
    full = lambda a: pl.BlockSpec(a.shape, lambda b, t: (0,) * a.ndim)
    ws = (lw["rg_conv_w"], lw["rg_conv_b"], lw["rg_w_r_bd"], lw["rg_b_r"], lw["rg_w_i_bd"], lw["rg_b_i"],
          lw["rg_lambda"])
    return pl.pallas_call(
        functools.partial(_rg_p_body, tile=tile), grid=(batch, nt),
        in_specs=[pl.BlockSpec((tile, RG_W), lambda b, t: (rb(b, t), C_RX // RG_W)),
                  pl.BlockSpec((tile, RG_W), lambda b, t: (rb(b, t), C_RGT // RG_W))] + [full(a) for a in ws],
        out_specs=[pl.BlockSpec((tile, RG_W), lambda b, t: (rb(b, t), 0)),
                   pl.BlockSpec((1, 1, RG_W), lambda b, t: (b, 0, 0)),
                   pl.BlockSpec((1, CONV_W - 1, RG_W), lambda b, t: (b, 0, 0))],
        out_shape=[jax.ShapeDtypeStruct((batch * seq_len, RG_W), F32),
                   jax.ShapeDtypeStruct((batch, 1, RG_W), F32),
                   jax.ShapeDtypeStruct((batch, CONV_W - 1, RG_W), F32)],
        scratch_shapes=[pltpu.VMEM((8 + tile, RG_W), F32), pltpu.VMEM((1, RG_W), F32)],
        compiler_params=_cp(("arbitrary", "arbitrary")), name="rglru_prompt")(z, z, *ws)


def _rg_s_body(x_ref, g_ref, b0_ref, b1_ref, b2_ref, h0_ref, cw_ref, cb_ref, wr_ref, br_ref, wi_ref, bi_ref,
               lam_ref, y_ref, h_ref):
    cw = cw_ref[...]
    x = x_ref[...]
    xc = cb_ref[...] + b0_ref[...] * cw[0:1] + b1_ref[...] * cw[1:2] + b2_ref[...] * cw[2:3] + x * cw[3:4]
    a, u = _rg_gates(xc, wr_ref, br_ref, wi_ref, bi_ref, lam_ref)
    h = u + a * h0_ref[...]
    h_ref[...] = h
    y_ref[...] = h * _gelu_tanh(g_ref[...])


def rglru_decode(z, conv_buf, h0, lw):
    s = z.shape[0]
    buf = conv_buf.reshape(s, (CONV_W - 1) * RG_W)
    full = lambda a: pl.BlockSpec(a.shape, lambda i: (0,) * a.ndim)
    ws = (lw["rg_conv_w"], lw["rg_conv_b"], lw["rg_w_r_bd"], lw["rg_b_r"], lw["rg_w_i_bd"], lw["rg_b_i"],
          lw["rg_lambda"])
    blk = lambda c: pl.BlockSpec((s, RG_W), lambda i: (0, c))
    y, h = pl.pallas_call(
        _rg_s_body, grid=(1,),
        in_specs=[blk(C_RX // RG_W), blk(C_RGT // RG_W), blk(0), blk(1), blk(2), blk(0)] + [full(a) for a in ws],
        out_specs=[blk(0), blk(0)],
        out_shape=[jax.ShapeDtypeStruct((s, RG_W), F32), jax.ShapeDtypeStruct((s, RG_W), F32)],
        compiler_params=_cp(("arbitrary",)), name="rglru_decode")(z, z, buf, buf, buf, h0, *ws)
    buf_new = jnp.concatenate([conv_buf[:, 1:], z[:, None, C_RX:C_RX + RG_W]], axis=1)
    return y, buf_new, h


def _cmp_weights(lg):
    e = jnp.exp(lg - jnp.max(lg, axis=-1, keepdims=True))
    return e / (jnp.sum(e, axis=-1, keepdims=True) * (CMP_BLOCK / lg.shape[-1]))


def _topk_mask(vals, lane, k):
    sel = jnp.zeros(vals.shape, jnp.bool_)
    idxs, oks = [], []
    lane = lane.astype(F32)
    big = float(vals.shape[-1])
    for _ in range(k):
        m = jnp.max(vals, axis=-1, keepdims=True)
        idx = jnp.min(jnp.where(vals == m, lane, big), axis=-1, keepdims=True)
        pick = lane == idx
        ok = m > -jnp.inf
        sel = sel | (pick & ok)
        vals = jnp.where(pick, -jnp.inf, vals)
        idxs.append(idx)
        oks.append(ok)
    return sel, idxs, oks


def _topk_rank_mask(vals, n, k):
    vt = vals.T[:n]
    cidx = lax.broadcasted_iota(jnp.int32, (n, 1), 0)
    rank = jnp.zeros(vt.shape, F32)
    for j in range(n):
        rj = vt[j:j + 1]
        ahead = (rj > vt) | ((rj == vt) & (cidx > j))
        rank = rank + jnp.where(ahead, 1.0, 0.0)
    sel_t = (rank < k) & (vt > -jnp.inf)
    return jnp.where(sel_t, 1.0, 0.0).T > 0.5


def _nsa_p_body(q_ref, g_ref, kvc_ref, kvs_ref, kvw_ref, lg_ref, o_ref,
                kcb_ref, vcb_ref, ks_ref, vs_ref, kw_ref, vw_ref, *, seq_len):
    i = pl.program_id(1)
    qb = QBLOCK
    n_cmp = seq_len // CMP_BLOCK
    tk = 512
    rows = NSA_H * qb

    @pl.when(i == 0)
    def _():
        wc = _cmp_weights(lg_ref[...])
        blk = lax.broadcasted_iota(jnp.int32, (n_cmp, seq_len), 0)
        pos = lax.broadcasted_iota(jnp.int32, (n_cmp, seq_len), 1)
        wmat = jnp.where((pos // CMP_BLOCK) == blk, wc, 0.0).astype(BF16)
        kvcb = _dot(wmat, kvc_ref[...].astype(BF16))
        kcb_ref[...] = kvcb[:, :NSA_D].astype(BF16)
        vcb_ref[...] = kvcb[:, NSA_D:].astype(BF16)
        kvs = kvs_ref[...]
        ks_ref[...] = kvs[:, :NSA_D].astype(BF16)
        vs_ref[...] = kvs[:, NSA_D:].astype(BF16)
        kvw = kvw_ref[...]
        kw_ref[...] = kvw[:, :NSA_D].astype(BF16)
        vw_ref[...] = kvw[:, NSA_D:].astype(BF16)

    q = q_ref[...] * NSA_SCALE
    q2 = jnp.concatenate([q[:, h * NSA_D:(h + 1) * NSA_D] for h in range(NSA_H)], axis=0).astype(BF16)
    t0 = i * qb
    tq = t0 + lax.broadcasted_iota(jnp.int32, (qb, 1), 0)
    slope3 = _alibi_slope(lax.broadcasted_iota(jnp.int32, (NSA_H, 1, 1), 0))

    lane_c = lax.broadcasted_iota(jnp.int32, (1, n_cmp), 1)
    dist_c = (tq - (lane_c * CMP_BLOCK + (CMP_BLOCK - 1))).astype(F32)
    s_c = _dot_nt(q2, kcb_ref[...]).reshape(NSA_H, qb, n_cmp) - slope3 * dist_c[None]
    p_c = _softmax_rows(s_c, (dist_c >= 0)[None])
    o_c = _dot(p_c.reshape(rows, n_cmp).astype(BF16), vcb_ref[...])

    imp = jnp.sum(p_c, axis=0)
    n_sel = seq_len // SEL_BLOCK
    pr = lax.broadcasted_iota(jnp.int32, (n_cmp, n_cmp), 0)
    pc = lax.broadcasted_iota(jnp.int32, (n_cmp, n_cmp), 1)
    pair = ((pr // (SEL_BLOCK // CMP_BLOCK)) == pc).astype(F32)
    imp = _dot(imp, pair, HIGHEST)
    forced = (lane_c == tq // SEL_BLOCK) | (lane_c == 0)
    causal = (lane_c * SEL_BLOCK <= tq) & (lane_c < n_sel)
    imp = jnp.where(forced, jnp.inf, jnp.where(causal, imp, -jnp.inf))
    sel_f = jnp.where(_topk_rank_mask(imp, n_sel, min(SEL_TOPK, n_sel)), 1.0, 0.0)
    sel_b = sel_f.astype(BF16)
    n_tiles = (t0 + qb + tk - 1) // tk
    blk_any = jnp.max(sel_f, axis=0, keepdims=True)
    blk_tile = (lax.broadcasted_iota(jnp.int32, (1, n_sel), 1) // (tk // SEL_BLOCK)).astype(F32)
    far = jnp.float32(seq_len // tk)
    j_lo = jnp.min(jnp.where((blk_any > 0.5) & (blk_tile >= 1.0), blk_tile, far)).astype(jnp.int32)
    j_lo = jnp.minimum(j_lo, n_tiles)

    def sel_tile(jj, carry):
        m, l, acc = carry
        j = jnp.where(jj == 0, 0, j_lo + jj - 1)
        k0 = pl.multiple_of(j * tk, tk)
        kt = ks_ref[pl.ds(k0, tk), :]
        vt = vs_ref[pl.ds(k0, tk), :]
        kpos = k0 + lax.broadcasted_iota(jnp.int32, (1, tk), 1)
        er = lax.broadcasted_iota(jnp.int32, (n_sel, tk), 0)
        ec = k0 + lax.broadcasted_iota(jnp.int32, (n_sel, tk), 1)
        expand = ((ec // SEL_BLOCK) == er).astype(BF16)
        ok = (_dot(sel_b, expand) > 0.5) & (kpos <= tq)
        dist = (tq - kpos).astype(F32)
        bias = jnp.where(ok, 0.0, NEG)
        s = _dot_nt(q2, kt).reshape(NSA_H, qb, tk) - slope3 * dist[None] + bias[None]
        s = s.reshape(rows, tk)
        m_new = jnp.maximum(m, jnp.max(s, axis=-1, keepdims=True))
        alpha = jnp.exp(m - m_new)
        p = jnp.exp(s - m_new)
        l = alpha * l + jnp.sum(p, axis=-1, keepdims=True)
        acc = alpha * acc + _dot(p.astype(BF16), vt)
        return m_new, l, acc

    m0 = jnp.full((rows, 1), NEG, F32)
    _, l_s, acc_s = lax.fori_loop(0, 1 + n_tiles - j_lo, sel_tile,
                                  (m0, jnp.zeros((rows, 1), F32), jnp.zeros((rows, NSA_D), F32)))
    o_s = acc_s / l_s

    wk = min(WINDOW + qb, seq_len)
    w0 = pl.multiple_of(jnp.maximum(t0 - WINDOW, 0), qb)
    wpos = w0 + lax.broadcasted_iota(jnp.int32, (1, wk), 1)
    dist_w = tq - wpos
    ok_w = (dist_w >= 0) & (dist_w <= WINDOW)
    s_w = _dot_nt(q2, kw_ref[pl.ds(w0, wk), :]).reshape(NSA_H, qb, wk) - slope3 * dist_w.astype(F32)[None]
    p_w = _softmax_rows(s_w, ok_w[None])
    o_w = _dot(p_w.reshape(rows, wk).astype(BF16), vw_ref[pl.ds(w0, wk), :])

    gi = _sigmoid(g_ref[...])
    outs = []
    for h in range(NSA_H):
        r = slice(h * qb, (h + 1) * qb)
        outs.append(gi[:, 3 * h:3 * h + 1] * o_c[r] + gi[:, 3 * h + 1:3 * h + 2] * o_s[r]
                    + gi[:, 3 * h + 2:3 * h + 3] * o_w[r])
    o_ref[...] = jnp.concatenate(outs, axis=1)


def nsa_prompt(z, lw, batch, seq_len):
    nq = seq_len // QBLOCK
    n_cmp = seq_len // CMP_BLOCK
    rb = lambda b, i: b * nq + i
    kv = lambda c: pl.BlockSpec((seq_len, LANE), lambda b, i: (b, c))
    lg = jnp.tile(lw["nsa_cmp_logits"], seq_len // CMP_BLOCK)[None]
    return pl.pallas_call(
        functools.partial(_nsa_p_body, seq_len=seq_len), grid=(batch, nq),
        in_specs=[pl.BlockSpec((QBLOCK, 512), lambda b, i: (rb(b, i), C_NQ // 512)),
                  pl.BlockSpec((QBLOCK, LANE), lambda b, i: (rb(b, i), C_NG // LANE)),
                  kv(C_NKV // LANE), kv(C_NKV // LANE + 1), kv(C_NKV // LANE + 2),
                  pl.BlockSpec((1, seq_len), lambda b, i: (0, 0))],
        out_specs=pl.BlockSpec((QBLOCK, 512), lambda b, i: (rb(b, i), 0)),
        out_shape=jax.ShapeDtypeStruct((batch * seq_len, 512), F32),
        scratch_shapes=[pltpu.VMEM((n_cmp, NSA_D), BF16), pltpu.VMEM((n_cmp, NSA_D), BF16)]
        + [pltpu.VMEM((seq_len, NSA_D), BF16)] * 4,
        compiler_params=_cp(("arbitrary", "arbitrary")), name="nsa_prompt")(z, z, z, z, z, lg)


def _mla_p_body(q_ref, k_ref, o_ref):
    i = pl.program_id(1)
    qb, tk = QBLOCK, 512
    rows = MLA_H * qb
    q2 = q_ref[...].reshape(rows, MLA_QW)
    tq = i * qb + (lax.broadcasted_iota(jnp.int32, (rows, 1), 0) & (qb - 1))

    def tile(j, carry, masked):
        m, l, acc = carry
        k0 = pl.multiple_of(j * tk, tk)
        kt = k_ref[pl.ds(k0, tk), :]
        s = _dot_nt(q2, kt)
        if masked:
            kpos = k0 + lax.broadcasted_iota(jnp.int32, (1, tk), 1)
            s = jnp.where(kpos <= tq, s, NEG)
        m_new = jnp.maximum(m, jnp.max(s, axis=-1, keepdims=True))
        alpha = jnp.exp(m - m_new)
        p = jnp.exp(s - m_new)
        l = alpha * l + jnp.sum(p, axis=-1, keepdims=True)
        acc = alpha * acc + _dot(p.astype(BF16), kt[:, :MLA_KVR])
        return m_new, l, acc

    n_full = (i * qb) // tk
    m0 = jnp.full((rows, 1), NEG, F32)
    carry = lax.fori_loop(0, n_full, functools.partial(tile, masked=False),
                          (m0, jnp.zeros((rows, 1), F32), jnp.zeros((rows, MLA_KVR), F32)))
    _, l, acc = tile(n_full, carry, True)
    o = (acc / l).astype(BF16)
    for h in range(MLA_H):
        o_ref[:, h * MLA_KVR:(h + 1) * MLA_KVR] = o[h * qb:(h + 1) * qb]


def mla_prompt(qm, km, batch, seq_len):
    nq = seq_len // QBLOCK
    return pl.pallas_call(
        _mla_p_body, grid=(batch, nq),
        in_specs=[pl.BlockSpec((MLA_H, QBLOCK, MLA_QW), lambda b, i: (0, b * nq + i, 0)),
                  pl.BlockSpec((seq_len, MLA_QW), lambda b, i: (b, 0))],
        out_specs=pl.BlockSpec((QBLOCK, MLA_H * MLA_KVR), lambda b, i: (b * nq + i, 0)),
        out_shape=jax.ShapeDtypeStruct((batch * seq_len, MLA_H * MLA_KVR), BF16),
        compiler_params=_cp(("arbitrary", "arbitrary")), name="mla_prompt")(qm, km)


CH_PAGES = 16
CH_ROWS = CH_PAGES * PAGE
MLA_CH_PAGES = 32
HPAD = 16
MLA_SLOTS = 4
CMP_SLOTS = 8


def _chunk_copies(cache_ref, layer, pt_ref, buf_ref, sem_ref, seq, chunk, slot, feats):
    copies = []
    ch_pages = buf_ref.shape[-1] // PAGE
    for p in range(ch_pages):
        page = pt_ref[seq, chunk * ch_pages + p]
        src = cache_ref.at[layer, page] if feats is None else cache_ref.at[layer, page, pl.ds(0, feats), :]
        copies.append(pltpu.make_async_copy(src, buf_ref.at[slot, :, pl.ds(p * PAGE, PAGE)], sem_ref.at[slot]))
    return copies


def _paged_pipeline(cache_ref, layer, pt_ref, buf_ref, sem_ref, n_chunks, feats, compute):
    s = pl.program_id(0)
    n_seq = pl.num_programs(0)
    n_slots = buf_ref.shape[0]
    ahead = n_slots - 1
    assert n_chunks % n_slots == 0
    args = (cache_ref, layer, pt_ref, buf_ref, sem_ref)

    @pl.when(s == 0)
    def _():
        for c0 in range(ahead):
            for cp in _chunk_copies(*args, 0, c0, c0, feats):
                cp.start()

    def body(c, carry):
        slot = c % n_slots
        tgt = c + ahead
        wrap = tgt >= n_chunks
        s2 = jnp.where(wrap, s + 1, s)
        c2 = jnp.where(wrap, tgt - n_chunks, tgt)

        @pl.when(s2 < n_seq)
        def _():
            for cp in _chunk_copies(*args, s2, c2, c2 % n_slots, feats):
                cp.start()

        for cp in _chunk_copies(*args, s, c, slot, feats):
            cp.wait()
        return compute(c, slot, carry)

    return body


def _nsa_cmp_body(pt_ref, q_ref, lg_ref, cache_ref, oc_ref, imp_ref, buf_ref, sem_ref, wm_ref, kv_ref,
                  *, layer, n_chunks, t_pos):
    s = pl.program_id(0)
    cmp_rows = CH_ROWS // CMP_BLOCK

    @pl.when(s == 0)
    def _():
        wc = _cmp_weights(lg_ref[...])
        blk = lax.broadcasted_iota(jnp.int32, (cmp_rows, CH_ROWS), 0)
        pos = lax.broadcasted_iota(jnp.int32, (cmp_rows, CH_ROWS), 1)
        wm_ref[...] = jnp.where((pos // CMP_BLOCK) == blk, wc, 0.0).astype(BF16)

    def compute(c, slot, carry):
        r0 = pl.multiple_of(c * cmp_rows, cmp_rows)
        kv_ref[pl.ds(r0, cmp_rows), :] = _dot_nt(wm_ref[...], buf_ref[slot].astype(BF16))
        return carry

    body = _paged_pipeline(cache_ref, layer, pt_ref, buf_ref, sem_ref, n_chunks, 2 * NSA_D, compute)
    lax.fori_loop(0, n_chunks, body, 0)

    n_cmp = n_chunks * cmp_rows
    kvcb = kv_ref[...]
    q2 = (q_ref[0] * NSA_SCALE).astype(BF16)
    hrow = lax.broadcasted_iota(jnp.int32, (HPAD, 1), 0)
    slope = _alibi_slope(jnp.minimum(hrow, NSA_H - 1))
    lane_c = lax.broadcasted_iota(jnp.int32, (1, n_cmp), 1)
    dist_c = (t_pos - (lane_c * CMP_BLOCK + (CMP_BLOCK - 1))).astype(F32)
    s_c = _dot_nt(q2, kvcb[:, :NSA_D].astype(BF16)) - slope * dist_c
    p_c = _softmax_rows(s_c, dist_c >= 0)
    oc_ref[0] = _dot(p_c.astype(BF16), kvcb[:, NSA_D:].astype(BF16))
    imp_ref[0] = jnp.sum(jnp.where(hrow < NSA_H, p_c, 0.0), axis=0, keepdims=True)


def nsa_decode_cmp(q16, page_table, cache4, lw, layer):
    n_seq, n_pages = page_table.shape
    n_chunks = n_pages // CH_PAGES
    n_cmp = n_pages * PAGE // CMP_BLOCK
    n_slots = min(CMP_SLOTS, n_chunks)
    lg = jnp.tile(lw["nsa_cmp_logits"], CH_ROWS // CMP_BLOCK)[None]
    gs = pltpu.PrefetchScalarGridSpec(
        num_scalar_prefetch=1, grid=(n_seq,),
        in_specs=[pl.BlockSpec((1, HPAD, NSA_D), lambda s, pt: (s, 0, 0)),
                  pl.BlockSpec((1, CH_ROWS), lambda s, pt: (0, 0)),
                  pl.BlockSpec(memory_space=pl.ANY)],
        out_specs=[pl.BlockSpec((1, HPAD, NSA_D), lambda s, pt: (s, 0, 0)),
                   pl.BlockSpec((1, 1, n_cmp), lambda s, pt: (s, 0, 0))],
        scratch_shapes=[pltpu.VMEM((n_slots, 2 * NSA_D, CH_ROWS), F32), pltpu.SemaphoreType.DMA((n_slots,)),
                        pltpu.VMEM((CH_ROWS // CMP_BLOCK, CH_ROWS), BF16), pltpu.VMEM((n_cmp, LANE), F32)])
    return pl.pallas_call(
        functools.partial(_nsa_cmp_body, layer=layer, n_chunks=n_chunks, t_pos=n_pages * PAGE),
        grid_spec=gs,
        out_shape=[jax.ShapeDtypeStruct((n_seq, HPAD, NSA_D), F32), jax.ShapeDtypeStruct((n_seq, 1, n_cmp), F32)],
        compiler_params=_cp(("arbitrary",)), name="nsa_decode_cmp")(page_table, q16, lg, cache4)


def _nsa_topk_body(imp_ref, idx_ref, *, t_pos):
    imp = imp_ref[...]
    n_seq, n_cmp = imp.shape
    n_sel = -(-(t_pos + 1) // SEL_BLOCK)
    width = idx_ref.shape[-1]
    pr = lax.broadcasted_iota(jnp.int32, (n_cmp, width), 0)
    pc = lax.broadcasted_iota(jnp.int32, (n_cmp, width), 1)
    pair = ((pr // (SEL_BLOCK // CMP_BLOCK)) == pc).astype(F32)
    vals = _dot(imp, pair, HIGHEST)
    lane = lax.broadcasted_iota(jnp.int32, (1, width), 1)
    forced = (lane == t_pos // SEL_BLOCK) | (lane == 0)
    causal = (lane * SEL_BLOCK <= t_pos) & (lane < n_sel)
    vals = jnp.where(forced, jnp.inf, jnp.where(causal, vals, -jnp.inf))
    _, idxs, oks = _topk_mask(vals, lane, min(SEL_TOPK, n_sel))
    out = jnp.full((n_seq, width), -1, jnp.int32)
    for r, (idx, ok) in enumerate(zip(idxs, oks)):
        out = jnp.where(lane == r, jnp.where(ok, idx.astype(jnp.int32), -1), out)
    idx_ref[...] = out


def nsa_decode_topk(imp, t_pos):
    n_seq, n_cmp = imp.shape
    width = 384
    return pl.pallas_call(
        functools.partial(_nsa_topk_body, t_pos=t_pos), grid=(1,),
        in_specs=[pl.BlockSpec((n_seq, n_cmp), lambda i: (0, 0))],
        out_specs=pl.BlockSpec((n_seq, width), lambda i: (0, 0)),
        out_shape=jax.ShapeDtypeStruct((n_seq, width), jnp.int32),
        compiler_params=_cp(("arbitrary",)), name="nsa_decode_topk")(imp)


def _sel_copies(cache_ref, layer, pt_ref, idx_ref, buf_ref, sem_ref, seq, slot, n_blk):
    per_page = PAGE // SEL_BLOCK
    copies = []
    for r in range(SEL_TOPK):
        blk = jnp.clip(idx_ref[seq, r], 0, n_blk - 1)
        page = pt_ref[seq, blk // per_page]
        src = cache_ref.at[layer, page, pl.ds(2 * NSA_D, 2 * NSA_D), :]
        copies.append(pltpu.make_async_copy(src, buf_ref.at[slot, :, pl.ds(r * PAGE, PAGE)], sem_ref.at[slot]))
    return copies


def _attend_with_new(q2, slope, kt_b, vt_b, dist, ok, q_f32, k_new, v_new):
    s = jnp.where(ok, _dot(q2, kt_b) - slope * dist, NEG)
    s_new = jnp.sum(q_f32 * k_new, axis=-1, keepdims=True)
    m = jnp.maximum(jnp.max(s, axis=-1, keepdims=True), s_new)
    e = jnp.exp(s - m)
    e_new = jnp.exp(s_new - m)
    d = jnp.sum(e, axis=-1, keepdims=True) + e_new
    return (_dot_nt(e.astype(BF16), vt_b) + e_new * v_new) / d


def _nsa_sel_body(pt_ref, idx_ref, q_ref, g_ref, oc_ref, new_ref, win_ref, cache_ref, o_ref, buf_ref, sem_ref,
                  *, layer, t_pos):
    s = pl.program_id(0)
    n_seq = pl.num_programs(0)
    n_blk = t_pos // SEL_BLOCK
    slot = s % 2
    args = (cache_ref, layer, pt_ref, idx_ref, buf_ref, sem_ref)

    @pl.when(s == 0)
    def _():
        for cp in _sel_copies(*args, 0, 0, n_blk):
            cp.start()

    @pl.when(s + 1 < n_seq)
    def _():
        for cp in _sel_copies(*args, s + 1, 1 - slot, n_blk):
            cp.start()

    for cp in _sel_copies(*args, s, slot, n_blk):
        cp.wait()

    q = q_ref[0] * NSA_SCALE
    q2 = q.astype(BF16)
    hrow = lax.broadcasted_iota(jnp.int32, (HPAD, 1), 0)
    slope = _alibi_slope(jnp.minimum(hrow, NSA_H - 1))
    new = new_ref[0]
    per_page = PAGE // SEL_BLOCK
    nk = SEL_TOPK * PAGE
    lane = lax.broadcasted_iota(jnp.int32, (1, nk), 1)
    blk_of = jnp.zeros((1, nk), jnp.int32)
    for r in range(SEL_TOPK):
        blk_of = jnp.where(lane // PAGE == r, idx_ref[s, r], blk_of)
    in_page = lane % PAGE
    pos = (blk_of // per_page) * PAGE + in_page
    ok = (blk_of >= 0) & (blk_of < n_blk) & (in_page // SEL_BLOCK == blk_of % per_page)
    kv = buf_ref[slot]
    o_s = _attend_with_new(q2, slope, kv[:NSA_D].astype(BF16), kv[NSA_D:].astype(BF16),
                           (t_pos - pos).astype(F32), ok, q, new[:, 2 * NSA_D:3 * NSA_D], new[:, 3 * NSA_D:4 * NSA_D])
    wkv = win_ref[0]
    wbuf = wkv.shape[1]
    lane_w = lax.broadcasted_iota(jnp.int32, (1, wbuf), 1)
    dist_w = wbuf - lane_w
    ok_w = dist_w <= WINDOW
    o_w = _attend_with_new(q2, slope, wkv[:NSA_D].astype(BF16), wkv[NSA_D:].astype(BF16),
                           dist_w.astype(F32), ok_w, q, new[:, 4 * NSA_D:5 * NSA_D], new[:, 5 * NSA_D:6 * NSA_D])
    gi = _sigmoid(g_ref[0])
    o_ref[0] = gi[:, 0:1] * oc_ref[0] + gi[:, 1:2] * o_s + gi[:, 2:3] * o_w


def nsa_decode_sel(q16, g16, o_c, new_rows, win, page_table, idx, cache4, layer):
    n_seq, n_pages = page_table.shape
    wbuf = win.shape[2]
    gs = pltpu.PrefetchScalarGridSpec(
        num_scalar_prefetch=2, grid=(n_seq,),
        in_specs=[pl.BlockSpec((1, HPAD, NSA_D), lambda s, pt, ix: (s, 0, 0)),
                  pl.BlockSpec((1, HPAD, 3), lambda s, pt, ix: (s, 0, 0)),
                  pl.BlockSpec((1, HPAD, NSA_D), lambda s, pt, ix: (s, 0, 0)),
                  pl.BlockSpec((1, 1, 512), lambda s, pt, ix: (s, 0, 0)),
                  pl.BlockSpec((1, 2 * NSA_D, wbuf), lambda s, pt, ix: (layer * n_seq + s, 0, 0)),
                  pl.BlockSpec(memory_space=pl.ANY)],
        out_specs=pl.BlockSpec((1, HPAD, NSA_D), lambda s, pt, ix: (s, 0, 0)),
        scratch_shapes=[pltpu.VMEM((2, 2 * NSA_D, SEL_TOPK * PAGE), F32), pltpu.SemaphoreType.DMA((2,))])
    return pl.pallas_call(
        functools.partial(_nsa_sel_body, layer=layer, t_pos=n_pages * PAGE),
        grid_spec=gs, out_shape=jax.ShapeDtypeStruct((n_seq, HPAD, NSA_D), F32),
        compiler_params=_cp(("arbitrary",)), name="nsa_decode_sel")(
            page_table, idx, q16, g16, o_c, new_rows, win, cache4)


def _mla_s_body(pt_ref, q_ref, new_ref, cache_ref, o_ref, buf_ref, sem_ref, *, layer, n_chunks):
    q = q_ref[0]
    width = MLA_KVR + MLA_DR
    qk = q[:, :width]

    def compute(c, slot, carry):
        m, l, acc = carry
        kt = buf_ref[slot].astype(BF16)
        s = _dot(qk, kt)
        m_new = jnp.maximum(m, jnp.max(s, axis=-1, keepdims=True))
        alpha = jnp.exp(m - m_new)
        p = jnp.exp(s - m_new)
        l = alpha * l + jnp.sum(p, axis=-1, keepdims=True)
        acc = alpha * acc + _dot_nt(p.astype(BF16), kt[:MLA_KVR])
        return m_new, l, acc

    body = _paged_pipeline(cache_ref, layer, pt_ref, buf_ref, sem_ref, n_chunks, None, compute)
    m, l, acc = lax.fori_loop(0, n_chunks, body, (jnp.full((HPAD, 1), NEG, F32), jnp.zeros((HPAD, 1), F32),
                                                  jnp.zeros((HPAD, MLA_KVR), F32)))
    k_new = new_ref[0].astype(F32)
    s_new = jnp.sum(q.astype(F32) * k_new, axis=-1, keepdims=True)
    m_new = jnp.maximum(m, s_new)
    alpha = jnp.exp(m - m_new)
    e_new = jnp.exp(s_new - m_new)
    o_ref[0] = (alpha * acc + e_new * k_new[:, :MLA_KVR]) / (alpha * l + e_new)


def mla_decode(q16, k_new, page_table, cache_mla, layer):
    n_seq, n_pages = page_table.shape
    width = MLA_KVR + MLA_DR
    ch_pages = min(MLA_CH_PAGES, n_pages)
    n_chunks = n_pages // ch_pages
    n_slots = min(MLA_SLOTS, n_chunks)
    gs = pltpu.PrefetchScalarGridSpec(
        num_scalar_prefetch=1, grid=(n_seq,),
        in_specs=[pl.BlockSpec((1, HPAD, MLA_QW), lambda s, pt: (s, 0, 0)),
                  pl.BlockSpec((1, 1, MLA_QW), lambda s, pt: (s, 0, 0)),
                  pl.BlockSpec(memory_space=pl.ANY)],
        out_specs=pl.BlockSpec((1, HPAD, MLA_KVR), lambda s, pt: (s, 0, 0)),
        scratch_shapes=[pltpu.VMEM((n_slots, width, ch_pages * PAGE), F32), pltpu.SemaphoreType.DMA((n_slots,))])
    return pl.pallas_call(
        functools.partial(_mla_s_body, layer=layer, n_chunks=n_chunks),
        grid_spec=gs, out_shape=jax.ShapeDtypeStruct((n_seq, HPAD, MLA_KVR), F32),
        compiler_params=_cp(("arbitrary",)), name="mla_decode")(page_table, q16, k_new, cache_mla)


def _merge_body(gla_ref, nsa_ref, rg_ref, lat_ref, mg_ref, x_ref, wuv_ref, wb_ref, wo_ref, g_ref, b_ref, o_ref):
    o_mla = _dot(lat_ref[...], wuv_ref[...])
    branches = (gla_ref[...], nsa_ref[...], rg_ref[...], o_mla)
    merged = None
    for n, br in enumerate(branches):
        proj = _dot(br.astype(BF16), wb_ref[n])
        term = _sigmoid(mg_ref[:, n * D_MODEL:(n + 1) * D_MODEL]) * proj
        merged = term if merged is None else merged + term
    y = DN_ALPHA * x_ref[...] + _dot(merged.astype(BF16), wo_ref[...])
    o_ref[...] = _layer_norm(y, g_ref[...], b_ref[...])


def merge(o_gla, o_nsa, o_rg, o_lat, z, x, lw):
    n = x.shape[0]
    tm = min(n, 256)
    row = lambda w: pl.BlockSpec((tm, w), lambda i: (i, 0))
    full = lambda a: pl.BlockSpec(a.shape, lambda i: (0,) * a.ndim)
    ws = (lw["w_uv_bd"], lw["w_branch"], lw["w_out"], lw["ln1_g"], lw["ln1_b"])
    return pl.pallas_call(
        _merge_body, grid=(n // tm,),
        in_specs=[row(512), row(512), row(512), row(MLA_H * MLA_KVR), row(N_BRANCH * D_MODEL), row(D_MODEL)]
        + [full(a) for a in ws],
        out_specs=row(D_MODEL), out_shape=jax.ShapeDtypeStruct((n, D_MODEL), F32),
        compiler_params=_cp(("parallel",)), name="merge")(o_gla, o_nsa, o_rg, o_lat, z, x, *ws)


def _ffn_body(x_ref, wu_ref, wd_ref, g_ref, b_ref, o_ref, xb_ref, acc_ref):
    f = pl.program_id(1)

    @pl.when(f == 0)
    def _():
        xb_ref[...] = x_ref[...].astype(BF16)
        acc_ref[...] = jnp.zeros_like(acc_ref)

    h = jnp.maximum(_dot(xb_ref[...], wu_ref[...]), 0.0)
    acc_ref[...] += _dot((h * h).astype(BF16), wd_ref[...])

    @pl.when(f == pl.num_programs(1) - 1)
    def _():
        o_ref[...] = _layer_norm(DN_ALPHA * x_ref[...] + acc_ref[...], g_ref[...], b_ref[...])


def ffn(x, lw):
    n = x.shape[0]
    tm, tf = min(n, 1024), 1024
    vec = pl.BlockSpec((1, D_MODEL), lambda i, f: (0, 0))
    return pl.pallas_call(
        _ffn_body, grid=(n // tm, D_FF // tf),
        in_specs=[pl.BlockSpec((tm, D_MODEL), lambda i, f: (i, 0)),
                  pl.BlockSpec((D_MODEL, tf), lambda i, f: (0, f)),
                  pl.BlockSpec((tf, D_MODEL), lambda i, f: (f, 0)), vec, vec],
        out_specs=pl.BlockSpec((tm, D_MODEL), lambda i, f: (i, 0)),
        out_shape=jax.ShapeDtypeStruct((n, D_MODEL), F32),
        scratch_shapes=[pltpu.VMEM((tm, D_MODEL), BF16), pltpu.VMEM((tm, D_MODEL), F32)],
        compiler_params=_cp(("parallel", "arbitrary")), name="ffn")(x, lw["w_up"], lw["w_down"], lw["ln2_g"],
                                                                      lw["ln2_b"])


def _pad_cols(a, width):
    return jnp.pad(a, ((0, 0), (0, width - a.shape[1])))


def _block_diag(blocks):
    n, r, c = blocks.shape
    eye = jnp.eye(n, dtype=blocks.dtype)
    return (eye[:, None, :, None] * blocks[:, :, None, :]).reshape(n * r, n * c)


def _layer_weights(l, w_in, gla_w_a2, gla_b_a, gla_norm_g, nsa_cmp_logits, rg_conv_w, rg_conv_b, rg_w_r, rg_b_r,
                   rg_w_i, rg_b_i, rg_lambda, mla_q_g, mla_w_uq, mla_kv_g, mla_w_uk, mla_w_uv, w_branch, w_out,
                   ln1_g, ln1_b, w_up, w_down, ln2_g, ln2_b):
    (gq, gk, gv, glr, gr, nq, nkv, ng, rx, rgt, mq, mkv, mkr, mg) = jnp.split(w_in[l], IN_SPLITS, axis=1)
    w_in_p = jnp.concatenate(
        [mg, gv, gr, nq, rx, rgt, _pad_cols(nkv, 512), _pad_cols(mq, 512), gq, gk, mkv,
         _pad_cols(glr, LANE), _pad_cols(ng, LANE), _pad_cols(mkr, LANE)], axis=1)
    w_in_p = _pad_cols(w_in_p, DZ).astype(BF16)
    uq = mla_w_uq[l].reshape(MLA_QR, MLA_H, MLA_DN + MLA_DR)
    uq_nope = uq[:, :, :MLA_DN].reshape(MLA_QR, MLA_H * MLA_DN)
    uq_rope = jnp.pad(uq[:, :, MLA_DN:], ((0, 0), (0, 0), (0, LANE - MLA_DR))).reshape(MLA_QR, MLA_H * LANE)
    pad_rows = lambda a: jnp.pad(a, ((0, 512 - MLA_QR), (0, 0)))
    a2p = jnp.pad(gla_w_a2[l], ((0, LANE - GLA_RANK), (0, 0)))
    return {
        "w_in_p": w_in_p,
        "gla_w_a2p": a2p,
        "gla_b_a": gla_b_a[l][None],
        "gla_norm_g": gla_norm_g[l].reshape(1, GLA_H * GLA_DV),
        "gla_w_a2h": a2p.reshape(LANE, GLA_H, GLA_DK).transpose(1, 0, 2),
        "gla_b_ah": gla_b_a[l].reshape(GLA_H, 1, GLA_DK),
        "gla_norm_gh": gla_norm_g[l].reshape(GLA_H, 1, GLA_DV),
        "nsa_cmp_logits": nsa_cmp_logits[l],
        "rg_conv_w": rg_conv_w[l], "rg_conv_b": rg_conv_b[l][None],
        "rg_w_r_bd": _block_diag(rg_w_r[l]).astype(BF16), "rg_b_r": rg_b_r[l][None],
        "rg_w_i_bd": _block_diag(rg_w_i[l]).astype(BF16), "rg_b_i": rg_b_i[l][None],
        "rg_lambda": rg_lambda[l][None],
        "mla_q_g": _pad_cols(mla_q_g[l][None], 512),
        "w_uq_nope": pad_rows(uq_nope).astype(BF16),
        "w_uq_rope": pad_rows(uq_rope).astype(BF16),
        "w_uk_bd": _block_diag(mla_w_uk[l].transpose(1, 2, 0)).astype(BF16),
        "mla_kv_g": mla_kv_g[l][None],
        "w_uv_bd": _block_diag(mla_w_uv[l].transpose(1, 0, 2)).astype(BF16),
        "w_branch": w_branch[l].astype(BF16), "w_out": w_out[l].astype(BF16),
        "ln1_g": ln1_g[l][None], "ln1_b": ln1_b[l][None],
        "w_up": w_up[l].astype(BF16), "w_down": w_down[l].astype(BF16),
        "ln2_g": ln2_g[l][None], "ln2_b": ln2_b[l][None],
    }


def _rope_tables(pos):
    half = MLA_DR // 2
    freq = ROPE_BASE ** (-jnp.arange(half, dtype=F32) / half)
    ang = pos.astype(F32)[:, None] * freq
    cos, sin = jnp.cos(ang), jnp.sin(ang)
    zero = jnp.zeros_like(cos)
    pad = lambda a, b: jnp.pad(jnp.concatenate([a, b], axis=1), ((0, 0), (0, LANE - MLA_DR)))
    return pad(cos, cos), pad(-sin, zero), pad(zero, sin)


def _prompt_layer(x, lw, tabs, batch, seq_len):
    z = in_proj(x, lw["w_in_p"])
    qm, mla_rows, km = attn_prep(z, tabs, lw, seq_len)
    o_gla, s_gla = gla_prompt(z, lw, batch, seq_len)
    o_rg, rg_h, rg_buf = rglru_prompt(z, lw, batch, seq_len)
    o_nsa = nsa_prompt(z, lw, batch, seq_len)
    o_lat = mla_prompt(qm, km, batch, seq_len)
    x = ffn(merge(o_gla, o_nsa, o_rg, o_lat, z, x, lw), lw)
    nkv = z[:, C_NKV:C_NKV + 6 * NSA_D].reshape(batch, seq_len, 6, NSA_D)
    keep = min(WINDOW, seq_len)
    states = (nkv[:, :, :4], nkv[:, seq_len - keep:, 4:], mla_rows.reshape(batch, seq_len, -1), s_gla,
              rg_h.reshape(batch, RG_W), rg_buf)
    return x, states


def _sample_layer(x, lw, tabs, layer, cache4, win3, win_l, cache_mla, page_table, s_gla, rg_buf, rg_h):
    n_seq = x.shape[0]
    t_pos = page_table.shape[1] * PAGE
    z = in_proj(x, lw["w_in_p"])
    qm, mla_rows, km = attn_prep(z, tabs, lw, 1)
    o_gla, s_gla_new = gla_decode(z, s_gla, lw)
    o_rg, rg_buf_new, rg_h_new = rglru_decode(z, rg_buf, rg_h, lw)
    pad_h = lambda a: jnp.pad(a, ((0, 0), (0, HPAD - a.shape[1]), (0, 0)))
    q16 = pad_h(z[:, C_NQ:C_NQ + NSA_H * NSA_D].reshape(n_seq, NSA_H, NSA_D))
    g16 = pad_h(z[:, C_NG:C_NG + 3 * NSA_H].reshape(n_seq, NSA_H, 3))
    o_c, imp = nsa_decode_cmp(q16, page_table, cache4, lw, layer)
    idx = nsa_decode_topk(imp.reshape(n_seq, -1), t_pos)[:, :SEL_TOPK]
    new_rows = z[:, None, C_NKV:C_NKV + 512]
    o_nsa = nsa_decode_sel(q16, g16, o_c, new_rows, win3, page_table, idx, cache4, layer)
    o_nsa = o_nsa[:, :NSA_H].reshape(n_seq, NSA_H * NSA_D)
    qm16 = pad_h(qm.transpose(1, 0, 2))
    o_lat = mla_decode(qm16, km[:, None], page_table, cache_mla, layer)
    o_lat = o_lat[:, :MLA_H].reshape(n_seq, MLA_H * MLA_KVR).astype(BF16)
    x = ffn(merge(o_gla, o_nsa, o_rg, o_lat, z, x, lw), lw)
    nkv = z[:, C_NKV:C_NKV + 6 * NSA_D].reshape(n_seq, 1, 6, NSA_D)
    wkv = jnp.concatenate([win_l, nkv[:, :, 4:]], axis=1)
    keep = min(WINDOW, wkv.shape[1])
    states = (nkv[:, :, :4], wkv[:, wkv.shape[1] - keep:], mla_rows.reshape(n_seq, 1, -1), s_gla_new, rg_h_new,
              rg_buf_new)
    return x, states


def kernel(x_prompt, x_sample, cache_nsa, cache_nsa_win, cache_mla, state_gla, state_rg_h, state_rg_conv,
           page_table, w_in, gla_w_a2, gla_b_a, gla_norm_g, nsa_cmp_logits, rg_conv_w, rg_conv_b, rg_w_r,
           rg_b_r, rg_w_i, rg_b_i, rg_lambda, mla_q_g, mla_w_uq, mla_kv_g, mla_w_uk, mla_w_uv, w_branch,
           w_out, ln1_g, ln1_b, w_up, w_down, ln2_g, ln2_b):
    batch, seq_len, _ = x_prompt.shape
    n_seq, dec_seq, _ = x_sample.shape
    assert dec_seq == 1 and seq_len % 512 == 0 and page_table.shape[1] % max(CH_PAGES, MLA_CH_PAGES) == 0
    past = page_table.shape[1] * PAGE
    depth = w_in.shape[0]
    params = (w_in, gla_w_a2, gla_b_a, gla_norm_g, nsa_cmp_logits, rg_conv_w, rg_conv_b, rg_w_r, rg_b_r, rg_w_i,
              rg_b_i, rg_lambda, mla_q_g, mla_w_uq, mla_kv_g, mla_w_uk, mla_w_uv, w_branch, w_out, ln1_g, ln1_b,
              w_up, w_down, ln2_g, ln2_b)
    tabs_p = _rope_tables(jnp.arange(seq_len, dtype=jnp.int32))
    tabs_s = _rope_tables(jnp.full((n_seq,), past, jnp.int32))
    n_phys = cache_nsa.shape[1]
    cache4 = cache_nsa.transpose(0, 1, 3, 4, 2).reshape(depth, n_phys, 4 * NSA_D, PAGE)
    cache_mla = cache_mla.transpose(0, 1, 3, 2)
    wbuf = cache_nsa_win.shape[2]
    win3 = cache_nsa_win.transpose(0, 1, 3, 4, 2).reshape(depth * n_seq, 2 * NSA_D, wbuf)
    yp = x_prompt.reshape(batch * seq_len, D_MODEL)
    ys = x_sample.reshape(n_seq, D_MODEL)
    st_p, st_s = [], []
    for l in range(depth):
        lw = _layer_weights(l, *params)
        yp, sp = _prompt_layer(yp, lw, tabs_p, batch, seq_len)
        ys, ss = _sample_layer(ys, lw, tabs_s, l, cache4, win3, cache_nsa_win[l], cache_mla, page_table,
                               state_gla[l], state_rg_conv[l], state_rg_h[l])
        st_p.append(sp)
        st_s.append(ss)
    stack = lambda sts, i: jnp.stack([s[i] for s in sts])
    return (yp.reshape(batch, seq_len, D_MODEL), ys.reshape(n_seq, 1, D_MODEL),
            stack(st_p, 0), stack(st_s, 0), stack(st_p, 1), stack(st_s, 1), stack(st_p, 2), stack(st_s, 2),
            stack(st_p, 3), stack(st_s, 3), stack(st_p, 4), stack(st_s, 4), stack(st_p, 5), stack(st_s, 5))
```
